```python
import math
import jax, jax.numpy as jnp
from jax import lax
import numpy as np

D_MODEL = 1024
BATCH = 8
SEQ = 2048
DEPTH = 1
DEC_BATCH = 128
DEC_SEQ = 8
PAST_LEN = 2048
PAGE_SIZE = 128

HEAD_DIM = 64
H_FOX = 8
H_DIFF = 4
V_DIFF = 2 * HEAD_DIM
W_FOX = H_FOX * HEAD_DIM
W_DIFF_QK = H_DIFF * 2 * HEAD_DIM
W_DIFF_V = H_DIFF * V_DIFF
D_FF = -(-8 * D_MODEL // (3 * 256)) * 256
Q_BLOCK = 128
ALPHA = (2 * DEPTH) ** 0.25
BETA = (8 * DEPTH) ** -0.25
LN_EPS = 1e-5
NEG_INF = -1e30
FORGET_BIAS_INIT = 3.0
COL_SPLITS = (W_FOX, W_FOX, W_FOX, H_FOX, W_DIFF_QK, W_DIFF_QK, W_DIFF_V, D_MODEL, D_MODEL)
D_IN = sum(COL_SPLITS)

kernel_name = "fox_diff_gated_hybrid_step"


def layer_norm(x, g, b):
    xf = x.astype(jnp.float32)
    mu = jnp.mean(xf, axis=-1, keepdims=True)
    var = jnp.mean(jnp.square(xf - mu), axis=-1, keepdims=True)
    return ((xf - mu) * lax.rsqrt(var + LN_EPS) * g + b).astype(x.dtype)


def rms_norm(x, g):
    xf = x.astype(jnp.float32)
    return (xf * lax.rsqrt(jnp.mean(jnp.square(xf), axis=-1, keepdims=True) + LN_EPS) * g).astype(x.dtype)


def split_cols(p):
    outs, start = [], 0
    for w in COL_SPLITS:
        outs.append(p[..., start:start + w])
        start += w
    return outs


def adaln(c, w_ada, b_ada):
    mod = c @ w_ada + b_ada
    return jnp.split(mod[:, None, :], 6, axis=-1)


def mixer_inputs(x, shift, scale, w_in, b_forget):
    B, T = x.shape[0], x.shape[1]
    h = x * (1 + scale) + shift
    qa, ka, va, fa, qb, kb, vb, ga, gb = split_cols(h @ w_in)
    qa = qa.reshape(B, T, H_FOX, HEAD_DIM)
    ka = ka.reshape(B, T, H_FOX, HEAD_DIM)
    va = va.reshape(B, T, H_FOX, HEAD_DIM)
    logf = jax.nn.log_sigmoid((fa + b_forget).astype(jnp.float32))
    qb = qb.reshape(B, T, H_DIFF, 2, HEAD_DIM)
    kb = kb.reshape(B, T, H_DIFF, 2, HEAD_DIM)
    vb = vb.reshape(B, T, H_DIFF, V_DIFF)
    return qa, ka, va, logf, qb, kb, vb, ga, gb


def fox_attention(q, k, v, cq, ck, pos_q, pos_k):
    B, Tq = q.shape[0], q.shape[1]
    s = jnp.einsum('bqhd,bkhd->bhqk', q, k).astype(jnp.float32) * (HEAD_DIM ** -0.5)
    s = s + (jnp.transpose(cq, (0, 2, 1))[..., :, None] - jnp.transpose(ck, (0, 2, 1))[..., None, :])
    mask = pos_k[None, :] <= pos_q[:, None]
    p = jax.nn.softmax(jnp.where(mask, s, NEG_INF), axis=-1)
    o = jnp.einsum('bhqk,bkhd->bqhd', p.astype(v.dtype), v)
    return o.reshape(B, Tq, W_FOX)


def diff_attention(q, k, v, lam, slopes, subln_gain, lambda_init, pos_q, pos_k):
    B, Tq = q.shape[0], q.shape[1]
    s = jnp.einsum('bqhid,bkhid->bihqk', q, k).astype(jnp.float32) * (HEAD_DIM ** -0.5)
    dist = (pos_q[:, None] - pos_k[None, :]).astype(jnp.float32)
    bias = -slopes[:, None, None] * dist
    mask = pos_k[None, :] <= pos_q[:, None]
    p = jax.nn.softmax(jnp.where(mask, s + bias, NEG_INF), axis=-1)
    a = p[:, 0] - lam * p[:, 1]
    o = jnp.einsum('bhqk,bkhe->bqhe', a.astype(v.dtype), v)
    o = rms_norm(o, subln_gain) * (1.0 - lambda_init)
    return o.reshape(B, Tq, W_DIFF_V)


def sweep_query_blocks(fn, q_args, pos_q):
    n_blk = pos_q.shape[0] // Q_BLOCK

    def one(i):
        start = i * Q_BLOCK
        sl = [lax.dynamic_slice_in_dim(a, start, Q_BLOCK, axis=1) for a in q_args]
        return fn(*sl, lax.dynamic_slice_in_dim(pos_q, start, Q_BLOCK, axis=0))

    out = lax.map(one, jnp.arange(n_blk))
    out = jnp.moveaxis(out, 0, 1)
    return out.reshape((out.shape[0], n_blk * Q_BLOCK) + out.shape[3:])


def gather_pages(cache, page_table):
    g = cache[page_table]
    return g.reshape((g.shape[0], g.shape[1] * g.shape[2]) + g.shape[3:])


def merge_and_ffn(x, oa, ob, ga, gb, gate1, shift2, scale2, gate2,
                  w_branch_a, w_branch_b, w_out, ln1_gain, ln1_bias,
                  w_ffn_gate, w_ffn_up, w_ffn_down, ln2_gain, ln2_bias):
    m = jax.nn.sigmoid(ga) * (oa @ w_branch_a) + jax.nn.sigmoid(gb) * (ob @ w_branch_b)
    x = layer_norm(ALPHA * x + gate1 * (m @ w_out), ln1_gain, ln1_bias)
    h = x * (1 + scale2) + shift2
    f = (jax.nn.silu(h @ w_ffn_gate) * (h @ w_ffn_up)) @ w_ffn_down
    return layer_norm(ALPHA * x + gate2 * f, ln2_gain, ln2_bias)


def setup_inputs(seed: int = 0) -> dict:
    key = jax.random.key(seed)
    ks = jax.random.split(key, 40)
    n_pages = PAST_LEN // PAGE_SIZE
    n_used = DEC_BATCH * n_pages
    n_phys = (5 * n_used) // 4

    def nrm(k, shape, scale=1.0):
        return jax.random.normal(k, shape, jnp.float32) * scale

    page_table = jax.random.permutation(ks[0], n_phys)[:n_used].reshape(DEC_BATCH, n_pages).astype(jnp.int32)
    return {
        "x_prompt": nrm(ks[1], (BATCH, SEQ, D_MODEL)),
        "x_sample": nrm(ks[2], (DEC_BATCH, DEC_SEQ, D_MODEL)),
        "c_prompt": nrm(ks[3], (BATCH, D_MODEL)),
        "c_sample": nrm(ks[4], (DEC_BATCH, D_MODEL)),
        "cache_k_fox": nrm(ks[5], (DEPTH, n_phys, PAGE_SIZE, H_FOX, HEAD_DIM)),
        "cache_v_fox": nrm(ks[6], (DEPTH, n_phys, PAGE_SIZE, H_FOX, HEAD_DIM)),
        "cache_logf_fox": jax.nn.log_sigmoid(FORGET_BIAS_INIT + nrm(ks[7], (DEPTH, n_phys, PAGE_SIZE, H_FOX))),
        "cache_k_diff": nrm(ks[8], (DEPTH, n_phys, PAGE_SIZE, H_DIFF, 2, HEAD_DIM)),
        "cache_v_diff": nrm(ks[9], (DEPTH, n_phys, PAGE_SIZE, H_DIFF, V_DIFF)),
        "page_table": page_table,
        "w_ada": nrm(ks[10], (DEPTH, D_MODEL, 6 * D_MODEL), 0.5 * D_MODEL ** -0.5),
        "b_ada": nrm(ks[11], (DEPTH, 6 * D_MODEL), 0.01),
        "w_in": nrm(ks[12], (DEPTH, D_MODEL, D_IN), D_MODEL ** -0.5),
        "b_forget": FORGET_BIAS_INIT + nrm(ks[13], (DEPTH, H_FOX), 0.1),
        "lambda_q1": nrm(ks[14], (DEPTH, HEAD_DIM), 0.1),
        "lambda_k1": nrm(ks[15], (DEPTH, HEAD_DIM), 0.1),
        "lambda_q2": nrm(ks[16], (DEPTH, HEAD_DIM), 0.1),
        "lambda_k2": nrm(ks[17], (DEPTH, HEAD_DIM), 0.1),
        "subln_gain": 1.0 + nrm(ks[18], (DEPTH, V_DIFF), 0.02),
        "w_branch_a": nrm(ks[19], (DEPTH, W_FOX, D_MODEL), BETA * W_FOX ** -0.5),
        "w_branch_b": nrm(ks[20], (DEPTH, W_DIFF_V, D_MODEL), BETA * W_DIFF_V ** -0.5),
        "w_out": nrm(ks[21], (DEPTH, D_MODEL, D_MODEL), BETA * D_MODEL ** -0.5),
        "ln1_gain": 1.0 + nrm(ks[22], (DEPTH, D_MODEL), 0.02),
        "ln1_bias": nrm(ks[23], (DEPTH, D_MODEL), 0.02),
        "w_ffn_gate": nrm(ks[24], (DEPTH, D_MODEL, D_FF), BETA * D_MODEL ** -0.5),
        "w_ffn_up": nrm(ks[25], (DEPTH, D_MODEL, D_FF), BETA * D_MODEL ** -0.5),
        "w_ffn_down": nrm(ks[26], (DEPTH, D_FF, D_MODEL), BETA * D_FF ** -0.5),
        "ln2_gain": 1.0 + nrm(ks[27], (DEPTH, D_MODEL), 0.02),
        "ln2_bias": nrm(ks[28], (DEPTH, D_MODEL), 0.02),
    }


def reference(x_prompt, x_sample, c_prompt, c_sample, cache_k_fox, cache_v_fox, cache_logf_fox,
              cache_k_diff, cache_v_diff, page_table, w_ada, b_ada, w_in, b_forget,
              lambda_q1, lambda_k1, lambda_q2, lambda_k2, subln_gain, w_branch_a, w_branch_b,
              w_out, ln1_gain, ln1_bias, w_ffn_gate, w_ffn_up, w_ffn_down, ln2_gain, ln2_bias):
    slopes = 2.0 ** (-8.0 * jnp.arange(1, H_DIFF + 1, dtype=jnp.float32) / H_DIFF)
    pos_p = jnp.arange(SEQ)
    pos_kd = jnp.arange(PAST_LEN + DEC_SEQ)
    pos_qd = PAST_LEN + jnp.arange(DEC_SEQ)

    xp, xs = x_prompt, x_sample
    kfp, vfp, lfp, kdp, vdp = [], [], [], [], []
    kfs, vfs, lfs, kds, vds = [], [], [], [], []
    for l in range(DEPTH):
        lambda_init = 0.8 - 0.6 * math.exp(-0.3 * l)
        lam = (jnp.exp(jnp.sum(lambda_q1[l].astype(jnp.float32) * lambda_k1[l].astype(jnp.float32)))
               - jnp.exp(jnp.sum(lambda_q2[l].astype(jnp.float32) * lambda_k2[l].astype(jnp.float32)))
               + lambda_init)
        ffn_w = (w_branch_a[l], w_branch_b[l], w_out[l], ln1_gain[l], ln1_bias[l],
                 w_ffn_gate[l], w_ffn_up[l], w_ffn_down[l], ln2_gain[l], ln2_bias[l])

        sh1, sc1, g1, sh2, sc2, g2 = adaln(c_prompt, w_ada[l], b_ada[l])
        qa, ka, va, logf, qb, kb, vb, ga, gb = mixer_inputs(xp, sh1, sc1, w_in[l], b_forget[l])
        c_cum = jnp.cumsum(logf, axis=1)
        oa = sweep_query_blocks(
            lambda q, cq, pq: fox_attention(q, ka, va, cq, c_cum, pq, pos_p), (qa, c_cum), pos_p)
        ob = sweep_query_blocks(
            lambda q, pq: diff_attention(q, kb, vb, lam, slopes, subln_gain[l], lambda_init, pq, pos_p),
            (qb,), pos_p)
        xp = merge_and_ffn(xp, oa, ob, ga, gb, g1, sh2, sc2, g2, *ffn_w)
        kfp.append(ka); vfp.append(va); lfp.append(logf); kdp.append(kb); vdp.append(vb)

        sh1, sc1, g1, sh2, sc2, g2 = adaln(c_sample, w_ada[l], b_ada[l])
        qa_d, ka_d, va_d, logf_d, qb_d, kb_d, vb_d, ga_d, gb_d = mixer_inputs(xs, sh1, sc1, w_in[l], b_forget[l])
        k_fox = jnp.concatenate([gather_pages(cache_k_fox[l], page_table), ka_d], axis=1)
        v_fox = jnp.concatenate([gather_pages(cache_v_fox[l], page_table), va_d], axis=1)
        c_past = jnp.cumsum(gather_pages(cache_logf_fox[l], page_table).astype(jnp.float32), axis=1)
        c_new = c_past[:, -1:, :] + jnp.cumsum(logf_d, axis=1)
        c_keys = jnp.concatenate([c_past, c_new], axis=1)
        oa_d = fox_attention(qa_d, k_fox, v_fox, c_new, c_keys, pos_qd, pos_kd)
        k_diff = jnp.concatenate([gather_pages(cache_k_diff[l], page_table), kb_d], axis=1)
        v_diff = jnp.concatenate([gather_pages(cache_v_diff[l], page_table), vb_d], axis=1)
        ob_d = diff_attention(qb_d, k_diff, v_diff, lam, slopes, subln_gain[l], lambda_init, pos_qd, pos_kd)
        xs = merge_and_ffn(xs, oa_d, ob_d, ga_d, gb_d, g1, sh2, sc2, g2, *ffn_w)
        kfs.append(ka_d); vfs.append(va_d); lfs.append(logf_d); kds.append(kb_d); vds.append(vb_d)

    return (xp, xs,
            jnp.stack(kfp, 0), jnp.stack(vfp, 0), jnp.stack(lfp, 0), jnp.stack(kdp, 0), jnp.stack(vdp, 0),
            jnp.stack(kfs, 0), jnp.stack(vfs, 0), jnp.stack(lfs, 0), jnp.stack(kds, 0), jnp.stack(vds, 0))
```

```python
import functools
import math

import jax
import jax.numpy as jnp
import numpy as np
from jax import lax
from jax.experimental import pallas as pl
from jax.experimental.pallas import tpu as pltpu

F32 = jnp.float32
BF16 = jnp.bfloat16

LANES = 128
SUBLANES = 8
HEAD_DIM = 64
H_FOX = 8
H_DIFF = 4
V_DIFF = 2 * HEAD_DIM
W_ATT = H_FOX * HEAD_DIM
LN_EPS = 1e-5
NEG_INF = -1e30
VMEM_LIMIT = 56 * 1024 * 1024

PROJ_ROWS = 512
ATT_BLOCK = 256
FFN_ROWS = 256
PAGES_PER_STEP = 4
SEQS_PER_SUFFIX_BLOCK = 8

_HI = lax.Precision.HIGHEST
_NT = (((1,), (1,)), ((), ()))


def _log2(n):
    assert n & (n - 1) == 0, n
    return n.bit_length() - 1


def _resident(shape):
    nd = len(shape)
    return pl.BlockSpec(shape, lambda *_: (0,) * nd, pipeline_mode=pl.Buffered(1))


def _params(sem):
    return pltpu.CompilerParams(dimension_semantics=sem, vmem_limit_bytes=VMEM_LIMIT)


def _ada_kernel(c_ref, w_ref, b_ref, o_ref):
    o_ref[0] = jnp.dot(c_ref[...].astype(BF16), w_ref[...].astype(BF16),
                       preferred_element_type=F32) + b_ref[...]


def _adaln(c_all, w_ada, b_ada):
    m, d = c_all.shape
    n_chunks = w_ada.shape[1] // d
    return pl.pallas_call(
        _ada_kernel,
        grid=(n_chunks,),
        in_specs=[pl.BlockSpec((m, d), lambda j: (0, 0)),
                  pl.BlockSpec((d, d), lambda j: (0, j)),
                  pl.BlockSpec((1, d), lambda j: (0, j))],
        out_specs=pl.BlockSpec((1, m, d), lambda j: (j, 0, 0)),
        out_shape=jax.ShapeDtypeStruct((n_chunks, m, d), F32),
        compiler_params=_params(("arbitrary",)),
        name="adaln",
    )(c_all, w_ada, b_ada.reshape(1, -1))


def _proj_kernel(x_ref, sc_ref, sh_ref, w_ref, bf_ref,
                 qa_ref, ka_ref, va_ref, qb_ref, kb_ref, vb_ref, ga_ref, gb_ref,
                 logf_ref, cum_ref, cumt_ref, *rest, group, carry_cum, emit_bf16):
    if emit_bf16:
        kab_ref, vab_ref, kbb_ref, vbb_ref, carry_ref = rest
    else:
        (carry_ref,) = rest
    nb, r, d = x_ref.shape
    rows = nb * r
    h = x_ref[...] * (1.0 + sc_ref[...]) + sh_ref[...]
    hb = h.reshape(rows, d).astype(BF16)

    def col(start, width):
        return jnp.dot(hb, w_ref[:, start:start + width], preferred_element_type=F32)

    scale = HEAD_DIM ** -0.5
    w = W_ATT
    qa_ref[...] = (col(0, w) * scale).astype(BF16).reshape(nb, r, w)
    ka = col(w, w)
    ka_ref[...] = ka.reshape(nb, r, w)
    va = col(2 * w, w)
    va_ref[...] = va.reshape(nb, r, w)
    qb_ref[...] = (col(3 * w, w) * scale).astype(BF16).reshape(nb, r, w)
    kb = col(4 * w, w)
    kb_ref[...] = kb.reshape(nb, r, w)
    vb = col(5 * w, w)
    vb_ref[...] = vb.reshape(nb, r, w)
    if emit_bf16:
        kab_ref[...] = ka.astype(BF16).reshape(nb, r, w)
        vab_ref[...] = va.astype(BF16).reshape(nb, r, w)
        kbb_ref[...] = kb.astype(BF16).reshape(nb, r, w)
        vbb_ref[...] = vb.astype(BF16).reshape(nb, r, w)
    ga_ref[...] = col(6 * w, d).reshape(nb, r, d)
    gb_ref[...] = col(6 * w + d, d).reshape(nb, r, d)

    fa = col(6 * w + 2 * d, LANES) + bf_ref[...]
    logf = jax.nn.log_sigmoid(fa)
    ri = lax.broadcasted_iota(jnp.int32, (rows, rows), 0)
    ci = lax.broadcasted_iota(jnp.int32, (rows, rows), 1)
    gs = group.bit_length() - 1
    tri = jnp.where((ci <= ri) & ((ri >> gs) == (ci >> gs)), 1.0, 0.0).astype(F32)
    cum = jnp.dot(tri, logf, precision=_HI, preferred_element_type=F32)
    if carry_cum:
        @pl.when(pl.program_id(1) == 0)
        def _():
            carry_ref[...] = jnp.zeros_like(carry_ref)
        cum = cum + carry_ref[...]
        carry_ref[...] = cum[rows - 1:rows, :]
    logf_ref[...] = logf[:, :H_FOX].reshape(nb, r, H_FOX)
    cum_ref[...] = cum[:, :H_FOX].reshape(nb, r, H_FOX)
    sel = jnp.where(lax.broadcasted_iota(jnp.int32, (SUBLANES, LANES), 0)
                    == lax.broadcasted_iota(jnp.int32, (SUBLANES, LANES), 1), 1.0, 0.0).astype(F32)
    cumt_ref[0] = lax.dot_general(sel, cum, _NT, precision=_HI, preferred_element_type=F32)


def _project(x, sc, sh, w_in_r, bf_pad, *, seqs_per_block, rows_per_seq_block, group, carry_cum,
             emit_bf16):
    n_seq, t, d = x.shape
    nb, r = seqs_per_block, rows_per_seq_block
    rows = nb * r
    grid = (n_seq // nb, t // r)
    w = W_ATT

    def blk(width):
        return pl.BlockSpec((nb, r, width), lambda i, j: (i, j, 0))

    mod = pl.BlockSpec((nb, 1, d), lambda i, j: (i, 0, 0))
    n_steps = grid[0] * grid[1]
    out_shape = [jax.ShapeDtypeStruct((n_seq, t, w), BF16),
                 jax.ShapeDtypeStruct((n_seq, t, w), F32),
                 jax.ShapeDtypeStruct((n_seq, t, w), F32),
                 jax.ShapeDtypeStruct((n_seq, t, w), BF16),
                 jax.ShapeDtypeStruct((n_seq, t, w), F32),
                 jax.ShapeDtypeStruct((n_seq, t, w), F32),
                 jax.ShapeDtypeStruct((n_seq, t, d), F32),
                 jax.ShapeDtypeStruct((n_seq, t, d), F32),
                 jax.ShapeDtypeStruct((n_seq, t, H_FOX), F32),
                 jax.ShapeDtypeStruct((n_seq, t, H_FOX), F32),
                 jax.ShapeDtypeStruct((n_steps, SUBLANES, rows), F32)]
    out_specs = [blk(w), blk(w), blk(w), blk(w), blk(w), blk(w), blk(d), blk(d),
                 blk(H_FOX), blk(H_FOX),
                 pl.BlockSpec((1, SUBLANES, rows), lambda i, j: (i * grid[1] + j, 0, 0))]
    if emit_bf16:
        out_shape += [jax.ShapeDtypeStruct((n_seq, t, w), BF16)] * 4
        out_specs += [blk(w)] * 4
    return pl.pallas_call(
        functools.partial(_proj_kernel, group=group, carry_cum=carry_cum, emit_bf16=emit_bf16),
        grid=grid,
        in_specs=[blk(d), mod, mod, _resident(w_in_r.shape), _resident(bf_pad.shape)],
        out_specs=out_specs,
        out_shape=out_shape,
        scratch_shapes=[pltpu.VMEM((1, LANES), F32)],
        compiler_params=_params(("arbitrary", "arbitrary")),
        name="project",
    )(x, sc, sh, w_in_r, bf_pad)


def _online_update(s, m, l, acc, v):
    m_new = jnp.maximum(m, jnp.max(s, axis=-1, keepdims=True))
    alpha = jnp.exp(m - m_new)
    p = jnp.exp(s - m_new)
    l_new = alpha * l + jnp.sum(p, axis=-1, keepdims=True)
    acc_new = alpha * acc + jnp.dot(p.astype(BF16), v, preferred_element_type=F32)
    return m_new, l_new, acc_new


def _fox_prompt_kernel(q_ref, k_ref, v_ref, cq_ref, ck_ref, o_ref, *, blk):
    i = pl.program_id(1)
    lane = lax.broadcasted_iota(jnp.int32, (blk, LANES), 1)
    lo = lane < HEAD_DIM
    ri = lax.broadcasted_iota(jnp.int32, (blk, blk), 0)
    ci = lax.broadcasted_iota(jnp.int32, (blk, blk), 1)
    causal = ci <= ri
    for pair in range(H_FOX // 2):
        cols = slice(pair * LANES, (pair + 1) * LANES)
        q2 = q_ref[0, :, cols].astype(F32)
        halves = []
        for half in range(2):
            hd = 2 * pair + half
            qh = jnp.where(lo if half == 0 else jnp.logical_not(lo), q2, 0.0).astype(BF16)
            cq = cq_ref[0, :, hd:hd + 1]

            def step(j, carry, masked, qh=qh, cq=cq, hd=hd, cols=cols):
                m, l, acc = carry
                start = pl.multiple_of(j * blk, blk)
                ks = k_ref[0, pl.ds(start, blk), cols]
                vs = v_ref[0, pl.ds(start, blk), cols]
                s = lax.dot_general(qh, ks, _NT, preferred_element_type=F32)
                s = s + (cq - ck_ref[0, hd, pl.ds(j, 1), :])
                if masked:
                    s = jnp.where(causal, s, NEG_INF)
                return _online_update(s, m, l, acc, vs)

            init = (jnp.full((blk, 1), NEG_INF, F32), jnp.zeros((blk, 1), F32),
                    jnp.zeros((blk, LANES), F32))
            carry = lax.fori_loop(0, i, functools.partial(step, masked=False), init)
            m, l, acc = step(i, carry, True)
            halves.append(acc / l)
        o_ref[0, :, cols] = jnp.where(lo, halves[0], halves[1]).astype(BF16)


def _fox_prompt(q, k, v, cum, cum_t, *, blk):
    b, t, w = q.shape
    nblk = t // blk
    return pl.pallas_call(
        functools.partial(_fox_prompt_kernel, blk=blk),
        grid=(b, nblk),
        in_specs=[pl.BlockSpec((1, blk, w), lambda bi, i: (bi, i, 0)),
                  pl.BlockSpec((1, t, w), lambda bi, i: (bi, 0, 0)),
                  pl.BlockSpec((1, t, w), lambda bi, i: (bi, 0, 0)),
                  pl.BlockSpec((1, blk, H_FOX), lambda bi, i: (bi, i, 0)),
                  pl.BlockSpec((1, H_FOX, nblk, blk), lambda bi, i: (bi, 0, 0, 0))],
        out_specs=pl.BlockSpec((1, blk, w), lambda bi, i: (bi, i, 0)),
        out_shape=jax.ShapeDtypeStruct((b, t, w), BF16),
        compiler_params=_params(("arbitrary", "arbitrary")),
        name="fox_prompt",
    )(q, k, v, cum, cum_t)


def _diff_lambda(lam_ref, lambda_init):
    lv = lam_ref[...]
    s1 = jnp.sum(lv[0:1] * lv[1:2], axis=-1, keepdims=True)
    s2 = jnp.sum(lv[2:3] * lv[3:4], axis=-1, keepdims=True)
    return jnp.exp(s1) - jnp.exp(s2) + lambda_init


def _subln(o, gain, lambda_init):
    ms = jnp.mean(o * o, axis=-1, keepdims=True)
    return o * lax.rsqrt(ms + LN_EPS) * gain * (1.0 - lambda_init)


def _alibi_slope(hd):
    return 2.0 ** (-8.0 * (hd + 1) / H_DIFF)


def _diff_prompt_kernel(q_ref, k_ref, v_ref, lam_ref, gain_ref, o_ref, *, blk, lambda_init):
    i = pl.program_id(1)
    lane = lax.broadcasted_iota(jnp.int32, (blk, LANES), 1)
    lo = lane < HEAD_DIM
    ri = lax.broadcasted_iota(jnp.int32, (blk, blk), 0)
    ci = lax.broadcasted_iota(jnp.int32, (blk, blk), 1)
    causal = ci <= ri
    rel = (ci - ri).astype(F32)
    lam = _diff_lambda(lam_ref, lambda_init)
    for hd in range(H_DIFF):
        cols = slice(hd * LANES, (hd + 1) * LANES)
        slope = _alibi_slope(hd)
        q2 = q_ref[0, :, cols].astype(F32)
        q_1 = jnp.where(lo, q2, 0.0).astype(BF16)
        q_2 = jnp.where(lo, 0.0, q2).astype(BF16)
        bias0 = slope * rel

        def step(j, carry, masked, q_1=q_1, q_2=q_2, bias0=bias0, slope=slope, cols=cols):
            c1, c2 = carry
            start = pl.multiple_of(j * blk, blk)
            ks = k_ref[0, pl.ds(start, blk), cols]
            vs = v_ref[0, pl.ds(start, blk), cols]
            bias = bias0 + (slope * blk) * (j - i).astype(F32)
            s1 = lax.dot_general(q_1, ks, _NT, preferred_element_type=F32) + bias
            s2 = lax.dot_general(q_2, ks, _NT, preferred_element_type=F32) + bias
            if masked:
                s1 = jnp.where(causal, s1, NEG_INF)
                s2 = jnp.where(causal, s2, NEG_INF)
            return _online_update(s1, *c1, vs), _online_update(s2, *c2, vs)

        def init():
            return (jnp.full((blk, 1), NEG_INF, F32), jnp.zeros((blk, 1), F32),
                    jnp.zeros((blk, LANES), F32))

        carry = lax.fori_loop(0, i, functools.partial(step, masked=False), (init(), init()))
        (m1, l1, a1), (m2, l2, a2) = step(i, carry, True)
        o = a1 / l1 - lam * (a2 / l2)
        o_ref[0, :, cols] = _subln(o, gain_ref[...], lambda_init).astype(BF16)


def _diff_prompt(q, k, v, lam_vecs, gain, *, blk, lambda_init):
    b, t, w = q.shape
    return pl.pallas_call(
        functools.partial(_diff_prompt_kernel, blk=blk, lambda_init=lambda_init),
        grid=(b, t // blk),
        in_specs=[pl.BlockSpec((1, blk, w), lambda bi, i: (bi, i, 0)),
                  pl.BlockSpec((1, t, w), lambda bi, i: (bi, 0, 0)),
                  pl.BlockSpec((1, t, w), lambda bi, i: (bi, 0, 0)),
                  pl.BlockSpec(lam_vecs.shape, lambda bi, i: (0, 0)),
                  pl.BlockSpec(gain.shape, lambda bi, i: (0, 0))],
        out_specs=pl.BlockSpec((1, blk, w), lambda bi, i: (bi, i, 0)),
        out_shape=jax.ShapeDtypeStruct((b, t, w), BF16),
        compiler_params=_params(("arbitrary", "arbitrary")),
        name="diff_prompt",
    )(q, k, v, lam_vecs, gain)


def _suffix_weights(page):
    l_in = np.arange(page * H_FOX)
    t_in, h_in = l_in // H_FOX, l_in % H_FOX
    l_out = np.arange(H_FOX * page)
    h_out, t_out = l_out // page, l_out % page
    same = h_in[:, None] == h_out[None, :]
    suffix = same & (t_in[:, None] > t_out[None, :])
    return np.concatenate([suffix, same], axis=1).astype(np.float32)


def _suffix_kernel(pt_ref, *refs, n_pages, seqs):
    page_refs = refs[:n_pages]
    w_ref, out_ref, x_scr, sw_scr = refs[n_pages:]
    b = pl.program_id(0)
    slot = b % seqs
    width = x_scr.shape[1]
    for p in range(n_pages):
        x_scr[pl.ds(slot * n_pages + p, 1), :] = page_refs[p][0]

    @pl.when(slot == seqs - 1)
    def _():
        x = x_scr[...]
        hi = x.astype(BF16)
        r1 = x - hi.astype(F32)
        mid = r1.astype(BF16)
        lo = (r1 - mid.astype(F32)).astype(BF16)
        wm = w_ref[...]
        sw_scr[...] = (jnp.dot(hi, wm, preferred_element_type=F32)
                       + jnp.dot(mid, wm, preferred_element_type=F32)
                       + jnp.dot(lo, wm, preferred_element_type=F32))
        for s in range(seqs):
            off = jnp.zeros((1, width), F32)
            for p in reversed(range(n_pages)):
                row = s * n_pages + p
                out_ref[s, p:p + 1, :] = sw_scr[row:row + 1, :width] + off
                off = off + sw_scr[row:row + 1, width:]


def _suffix_sums(cache_logf_l, page_table):
    n_phys, page, hf = cache_logf_l.shape
    n_seq, n_pages = page_table.shape
    width = page * hf
    flat = cache_logf_l.reshape(n_phys, 1, width)
    seqs = SEQS_PER_SUFFIX_BLOCK
    wmat = jnp.asarray(_suffix_weights(page), dtype=BF16)
    page_specs = [pl.BlockSpec((1, 1, width), functools.partial(lambda b, pt, p: (pt[b, p], 0, 0), p=p))
                  for p in range(n_pages)]
    grid_spec = pltpu.PrefetchScalarGridSpec(
        num_scalar_prefetch=1,
        grid=(n_seq,),
        in_specs=page_specs + [pl.BlockSpec(wmat.shape, lambda b, pt: (0, 0))],
        out_specs=pl.BlockSpec((seqs, n_pages, width), lambda b, pt: (b // seqs, 0, 0)),
        scratch_shapes=[pltpu.VMEM((seqs * n_pages, width), F32),
                        pltpu.VMEM((seqs * n_pages, 2 * width), F32)],
    )
    return pl.pallas_call(
        functools.partial(_suffix_kernel, n_pages=n_pages, seqs=seqs),
        grid_spec=grid_spec,
        out_shape=jax.ShapeDtypeStruct((n_seq, n_pages, width), F32),
        compiler_params=_params(("arbitrary",)),
        name="suffix_sums",
    )(page_table, *([flat] * n_pages), wmat)


def _block_diag_queries(q):
    nq, w = q.shape
    groups = w // HEAD_DIM
    tiled = jnp.broadcast_to(q.astype(F32)[None], (groups, nq, w)).reshape(groups * nq, w)
    ri = lax.broadcasted_iota(jnp.int32, (groups * nq, w), 0)
    ci = lax.broadcasted_iota(jnp.int32, (groups * nq, w), 1)
    return jnp.where((ri >> _log2(nq)) == (ci >> _log2(HEAD_DIM)), tiled, 0.0).astype(BF16)


def _decode_kernel(pt_ref, *refs, pps, page, past_len, lambda_init):
    n_cache = 4 * pps
    kf_refs = refs[0:pps]
    vf_refs = refs[pps:2 * pps]
    kd_refs = refs[2 * pps:3 * pps]
    vd_refs = refs[3 * pps:4 * pps]
    (qa_ref, ka_ref, va_ref, qb_ref, kb_ref, vb_ref, cq_ref, ckn_ref, r_ref, lam_ref, gain_ref,
     oa_ref, ob_ref,
     qf_scr, qd_scr, mf_scr, lf_scr, af_scr, md_scr, ld_scr, ad_scr) = refs[n_cache:]
    j = pl.program_id(1)
    n_steps = pl.num_programs(1)
    nq = qa_ref.shape[1]
    rows = H_FOX * nq
    chunk = pps * page

    @pl.when(j == 0)
    def _():
        qf_scr[...] = _block_diag_queries(qa_ref[0])
        qd_scr[...] = _block_diag_queries(qb_ref[0])
        for m_scr, l_scr, a_scr in ((mf_scr, lf_scr, af_scr), (md_scr, ld_scr, ad_scr)):
            m_scr[...] = jnp.full(m_scr.shape, NEG_INF, F32)
            l_scr[...] = jnp.zeros(l_scr.shape, F32)
            a_scr[...] = jnp.zeros(a_scr.shape, F32)

    def update(s, v, m_scr, l_scr, a_scr):
        m, l, acc = _online_update(s, m_scr[...], l_scr[...], a_scr[...], v)
        m_scr[...] = m
        l_scr[...] = l
        a_scr[...] = acc

    def stack(page_refs):
        return jnp.concatenate([pr[0].astype(BF16) for pr in page_refs], axis=0)

    row = lax.broadcasted_iota(jnp.int32, (rows, 1), 0)
    q_of_row = (row & (nq - 1)).astype(F32)
    slope_row = jnp.zeros((rows, 1), F32)
    for hd in range(H_DIFF):
        slope_row = jnp.where((row >> _log2(2 * nq)) == hd, _alibi_slope(hd), slope_row)

    s_f = lax.dot_general(qf_scr[...], stack(kf_refs), _NT, preferred_element_type=F32)
    r_tile = jnp.concatenate(
        [jnp.concatenate(
            [jnp.broadcast_to(r_ref[0, j, i:i + 1, hd * page:(hd + 1) * page], (nq, page))
             for i in range(pps)], axis=1)
         for hd in range(H_FOX)], axis=0)
    s_f = s_f + (cq_ref[0] + r_tile)
    update(s_f, stack(vf_refs), mf_scr, lf_scr, af_scr)

    s_d = lax.dot_general(qd_scr[...], stack(kd_refs), _NT, preferred_element_type=F32)
    key_pos = (lax.broadcasted_iota(jnp.int32, (rows, chunk), 1) + j * chunk).astype(F32)
    s_d = s_d + slope_row * (key_pos - (past_len + q_of_row))
    update(s_d, stack(vd_refs), md_scr, ld_scr, ad_scr)

    @pl.when(j == n_steps - 1)
    def _():
        pad = jnp.zeros((LANES - nq, W_ATT), F32)

        def padded(ref):
            return jnp.concatenate([ref[0], pad], axis=0).astype(BF16)

        col = lax.broadcasted_iota(jnp.int32, (rows, LANES), 1)
        visible = col <= (row & (nq - 1))
        s_n = lax.dot_general(qf_scr[...], padded(ka_ref), _NT, preferred_element_type=F32)
        s_n = jnp.where(visible, s_n + (cq_ref[0] - ckn_ref[0]), NEG_INF)
        update(s_n, padded(va_ref), mf_scr, lf_scr, af_scr)
        s_n = lax.dot_general(qd_scr[...], padded(kb_ref), _NT, preferred_element_type=F32)
        s_n = jnp.where(visible, s_n + slope_row * (col.astype(F32) - q_of_row), NEG_INF)
        update(s_n, padded(vb_ref), md_scr, ld_scr, ad_scr)

        of = af_scr[...] / lf_scr[...]
        lane = lax.broadcasted_iota(jnp.int32, (nq, W_ATT), 1)
        oa = jnp.zeros((nq, W_ATT), F32)
        for hd in range(H_FOX):
            oa = jnp.where((lane >> _log2(HEAD_DIM)) == hd, of[hd * nq:(hd + 1) * nq, :], oa)
        oa_ref[0] = oa.astype(BF16)

        od = ad_scr[...] / ld_scr[...]
        lam = _diff_lambda(lam_ref, lambda_init)
        outs = []
        for hd in range(H_DIFF):
            r0 = hd * 2 * nq
            cols = slice(hd * V_DIFF, (hd + 1) * V_DIFF)
            o = od[r0:r0 + nq, cols] - lam * od[r0 + nq:r0 + 2 * nq, cols]
            outs.append(_subln(o, gain_ref[...], lambda_init))
        ob_ref[0] = jnp.concatenate(outs, axis=1).astype(BF16)


def _decode_attention(page_table, caches, qa, ka, va, qb, kb, vb, cq_col, ck_new, r_past,
                      lam_vecs, gain, *, lambda_init):
    n_seq, nq, w = qa.shape
    n_pages = page_table.shape[1]
    pps = PAGES_PER_STEP
    page = caches[0].shape[1]
    rows = H_FOX * nq

    def page_spec(i):
        return pl.BlockSpec((1, page, w),
                            functools.partial(lambda b, j, pt, i: (pt[b, j * pps + i], 0, 0), i=i))

    def per_seq(shape):
        nd = len(shape)
        return pl.BlockSpec((1,) + tuple(shape[1:]), lambda b, j, pt: (b,) + (0,) * (nd - 1))

    cache_specs = [page_spec(i) for _ in range(4) for i in range(pps)]
    cache_args = [c for c in caches for _ in range(pps)]
    r_steps = r_past.reshape(n_seq, n_pages // pps, pps, r_past.shape[-1])
    small = [qa, ka, va, qb, kb, vb, cq_col, ck_new, r_steps]
    grid_spec = pltpu.PrefetchScalarGridSpec(
        num_scalar_prefetch=1,
        grid=(n_seq, n_pages // pps),
        in_specs=cache_specs + [per_seq(a.shape) for a in small]
        + [pl.BlockSpec(lam_vecs.shape, lambda b, j, pt: (0, 0)),
           pl.BlockSpec(gain.shape, lambda b, j, pt: (0, 0))],
        out_specs=[per_seq(qa.shape), per_seq(qa.shape)],
        scratch_shapes=[pltpu.VMEM((rows, w), BF16), pltpu.VMEM((rows, w), BF16),
                        pltpu.VMEM((rows, 1), F32), pltpu.VMEM((rows, 1), F32),
                        pltpu.VMEM((rows, w), F32),
                        pltpu.VMEM((rows, 1), F32), pltpu.VMEM((rows, 1), F32),
                        pltpu.VMEM((rows, w), F32)],
    )
    return pl.pallas_call(
        functools.partial(_decode_kernel, pps=pps, page=page, past_len=float(n_pages * page),
                          lambda_init=lambda_init),
        grid_spec=grid_spec,
        out_shape=[jax.ShapeDtypeStruct(qa.shape, BF16)] * 2,
        compiler_params=_params(("arbitrary", "arbitrary")),
        name="decode_attention",
    )(page_table, *cache_args, *small, lam_vecs, gain)


def _layer_norm(x, g, b):
    mu = jnp.mean(x, axis=-1, keepdims=True)
    xc = x - mu
    var = jnp.mean(xc * xc, axis=-1, keepdims=True)
    return xc * lax.rsqrt(var + LN_EPS) * g + b


def _ffn_kernel(x_ref, oa_ref, ob_ref, ga_ref, gb_ref, g1_ref, sh2_ref, sc2_ref, g2_ref,
                wa_ref, wb_ref, wo_ref, ln1g_ref, ln1b_ref, wg_ref, wu_ref, wd_ref,
                ln2g_ref, ln2b_ref, y_ref, *, alpha):
    nb, r, d = x_ref.shape
    rows = nb * r

    def flat(ref):
        return ref[...].reshape(rows, ref.shape[-1])

    def per_seq(ref):
        return jnp.broadcast_to(ref[...], (nb, r, d)).reshape(rows, d)

    a = jnp.dot(flat(oa_ref), wa_ref[...], preferred_element_type=F32)
    b = jnp.dot(flat(ob_ref), wb_ref[...], preferred_element_type=F32)
    m = jax.nn.sigmoid(flat(ga_ref)) * a + jax.nn.sigmoid(flat(gb_ref)) * b
    mo = jnp.dot(m.astype(BF16), wo_ref[...], preferred_element_type=F32)
    x1 = _layer_norm(alpha * flat(x_ref) + per_seq(g1_ref) * mo, ln1g_ref[...], ln1b_ref[...])
    h2 = (x1 * (1.0 + per_seq(sc2_ref)) + per_seq(sh2_ref)).astype(BF16)
    g = jnp.dot(h2, wg_ref[...], preferred_element_type=F32)
    u = jnp.dot(h2, wu_ref[...], preferred_element_type=F32)
    act = (g * jax.nn.sigmoid(g) * u).astype(BF16)
    f = jnp.dot(act, wd_ref[...], preferred_element_type=F32)
    y = _layer_norm(alpha * x1 + per_seq(g2_ref) * f, ln2g_ref[...], ln2b_ref[...])
    y_ref[...] = y.reshape(nb, r, d)


def _merge_ffn(x, oa, ob, ga, gb, g1, sh2, sc2, g2, weights, *, seqs_per_block, rows_per_seq_block,
               alpha):
    n_seq, t, d = x.shape
    nb, r = seqs_per_block, rows_per_seq_block

    def blk(width):
        return pl.BlockSpec((nb, r, width), lambda i, j: (i, j, 0))

    mod = pl.BlockSpec((nb, 1, d), lambda i, j: (i, 0, 0))
    return pl.pallas_call(
        functools.partial(_ffn_kernel, alpha=alpha),
        grid=(n_seq // nb, t // r),
        in_specs=[blk(d), blk(W_ATT), blk(W_ATT), blk(d), blk(d), mod, mod, mod, mod]
        + [_resident(wt.shape) for wt in weights],
        out_specs=blk(d),
        out_shape=jax.ShapeDtypeStruct((n_seq, t, d), F32),
        compiler_params=_params(("arbitrary", "arbitrary")),
        name="merge_ffn",
    )(x, oa, ob, ga, gb, g1, sh2, sc2, g2, *weights)


def kernel(x_prompt, x_sample, c_prompt, c_sample, cache_k_fox, cache_v_fox, cache_logf_fox,
           cache_k_diff, cache_v_diff, page_table, w_ada, b_ada, w_in, b_forget,
           lambda_q1, lambda_k1, lambda_q2, lambda_k2, subln_gain, w_branch_a, w_branch_b,
           w_out, ln1_gain, ln1_bias, w_ffn_gate, w_ffn_up, w_ffn_down, ln2_gain, ln2_bias):
    depth = w_in.shape[0]
    batch, seq, d = x_prompt.shape
    dec_batch, dec_seq, _ = x_sample.shape
    n_phys, page = cache_k_fox.shape[1], cache_k_fox.shape[2]
    alpha = (2 * depth) ** 0.25
    w = W_ATT
    f_start = 3 * w

    xp, xs = x_prompt, x_sample
    outs_p = [[] for _ in range(5)]
    outs_s = [[] for _ in range(5)]
    for l in range(depth):
        lambda_init = 0.8 - 0.6 * math.exp(-0.3 * l)
        w_in_r = jnp.concatenate(
            [w_in[l][:, :f_start], w_in[l][:, f_start + H_FOX:], w_in[l][:, f_start:f_start + H_FOX],
             jnp.zeros((d, LANES - H_FOX), F32)], axis=1).astype(BF16)
        bf_pad = jnp.concatenate([b_forget[l], jnp.zeros((LANES - H_FOX,), F32)]).reshape(1, LANES)
        lam_vecs = jnp.stack([lambda_q1[l], lambda_k1[l], lambda_q2[l], lambda_k2[l]]).astype(F32)
        gain = subln_gain[l].reshape(1, V_DIFF)
        ffn_w = (w_branch_a[l].astype(BF16), w_branch_b[l].astype(BF16), w_out[l].astype(BF16),
                 ln1_gain[l].reshape(1, d), ln1_bias[l].reshape(1, d),
                 w_ffn_gate[l].astype(BF16), w_ffn_up[l].astype(BF16), w_ffn_down[l].astype(BF16),
                 ln2_gain[l].reshape(1, d), ln2_bias[l].reshape(1, d))

        mod = _adaln(jnp.concatenate([c_prompt, c_sample], axis=0), w_ada[l], b_ada[l])
        mod_p = [mod[k, :batch].reshape(batch, 1, d) for k in range(6)]
        mod_s = [mod[k, batch:].reshape(dec_batch, 1, d) for k in range(6)]

        sh1, sc1, g1, sh2, sc2, g2 = mod_p
        (qa, ka, va, qb, kb, vb, ga, gb, logf, cum, cum_t, ka_b, va_b, kb_b, vb_b) = _project(
            xp, sc1, sh1, w_in_r, bf_pad, seqs_per_block=1, rows_per_seq_block=PROJ_ROWS,
            group=seq, carry_cum=True, emit_bf16=True)
        nblk = seq // ATT_BLOCK
        cum_rows = cum_t.reshape(batch, seq // PROJ_ROWS, SUBLANES, PROJ_ROWS)
        cum_rows = cum_rows.transpose(0, 2, 1, 3).reshape(batch, SUBLANES, nblk, ATT_BLOCK)
        oa = _fox_prompt(qa, ka_b, va_b, cum, cum_rows, blk=ATT_BLOCK)
        ob = _diff_prompt(qb, kb_b, vb_b, lam_vecs, gain, blk=ATT_BLOCK, lambda_init=lambda_init)
        xp = _merge_ffn(xp, oa, ob, ga, gb, g1, sh2, sc2, g2, ffn_w,
                        seqs_per_block=1, rows_per_seq_block=FFN_ROWS, alpha=alpha)
        for lst, val in zip(outs_p, (ka.reshape(batch, seq, H_FOX, HEAD_DIM),
                                     va.reshape(batch, seq, H_FOX, HEAD_DIM), logf,
                                     kb.reshape(batch, seq, H_DIFF, 2, HEAD_DIM),
                                     vb.reshape(batch, seq, H_DIFF, V_DIFF))):
            lst.append(val)

        sh1, sc1, g1, sh2, sc2, g2 = mod_s
        seqs_blk = PROJ_ROWS // dec_seq
        (qa, ka, va, qb, kb, vb, ga, gb, logf, cum, _unused) = _project(
            xs, sc1, sh1, w_in_r, bf_pad, seqs_per_block=seqs_blk, rows_per_seq_block=dec_seq,
            group=dec_seq, carry_cum=False, emit_bf16=False)
        cum_hq = cum.transpose(0, 2, 1)
        cq_col = cum_hq.reshape(dec_batch, H_FOX * dec_seq, 1)
        ck_new = jnp.repeat(cum_hq, dec_seq, axis=1)
        ck_new = jnp.pad(ck_new, ((0, 0), (0, 0), (0, LANES - dec_seq)))
        r_past = _suffix_sums(cache_logf_fox[l], page_table)
        caches = [c[l].reshape(n_phys, page, w)
                  for c in (cache_k_fox, cache_v_fox, cache_k_diff, cache_v_diff)]
        oa, ob = _decode_attention(page_table, caches, qa, ka, va, qb, kb, vb, cq_col, ck_new,
                                   r_past, lam_vecs, gain, lambda_init=lambda_init)
        ffn_seqs = FFN_ROWS // dec_seq
        xs = _merge_ffn(xs, oa, ob, ga, gb, g1, sh2, sc2, g2, ffn_w,
                        seqs_per_block=ffn_seqs, rows_per_seq_block=dec_seq, alpha=alpha)
        for lst, val in zip(outs_s, (ka.reshape(dec_batch, dec_seq, H_FOX, HEAD_DIM),
                                     va.reshape(dec_batch, dec_seq, H_FOX, HEAD_DIM), logf,
                                     kb.reshape(dec_batch, dec_seq, H_DIFF, 2, HEAD_DIM),
                                     vb.reshape(dec_batch, dec_seq, H_DIFF, V_DIFF))):
            lst.append(val)

    return (xp, xs, *[jnp.stack(o, 0) for o in outs_p], *[jnp.stack(o, 0) for o in outs_s])
```

```python
import functools
import math

import jax
import jax.numpy as jnp
import numpy as np
from jax import lax
from jax.experimental import pallas as pl
from jax.experimental.pallas import tpu as pltpu

F32 = jnp.float32
BF16 = jnp.bfloat16

LANES = 128
SUBLANES = 8
HEAD_DIM = 64
H_FOX = 8
H_DIFF = 4
V_DIFF = 2 * HEAD_DIM
W_ATT = H_FOX * HEAD_DIM
LN_EPS = 1e-5
NEG_INF = -1e30
VMEM_LIMIT = 56 * 1024 * 1024

PROJ_ROWS = 512
ATT_BLOCK = 256
ATT_HEADS_PER_LOOP = 4
FFN_ROWS = 256
PAGES_PER_STEP = 4

_HI = lax.Precision.HIGHEST
_NT = (((1,), (1,)), ((), ()))


def _log2(n):
    assert n & (n - 1) == 0, n
    return n.bit_length() - 1


def _resident(shape):
    nd = len(shape)
    return pl.BlockSpec(shape, lambda *_: (0,) * nd, pipeline_mode=pl.Buffered(1))


def _params(sem):
    return pltpu.CompilerParams(dimension_semantics=sem, vmem_limit_bytes=VMEM_LIMIT)


def _split3(x):
    hi = x.astype(BF16)
    r1 = x - hi.astype(F32)
    mid = r1.astype(BF16)
    lo = (r1 - mid.astype(F32)).astype(BF16)
    return hi, mid, lo


def _ada_kernel(c_ref, w_ref, b_ref, o_ref):
    o_ref[0] = jnp.dot(c_ref[...].astype(BF16), w_ref[...].astype(BF16),
                       preferred_element_type=F32) + b_ref[...]


def _adaln(c_all, w_ada, b_ada):
    m, d = c_all.shape
    n_chunks = w_ada.shape[1] // d
    return pl.pallas_call(
        _ada_kernel,
        grid=(n_chunks,),
        in_specs=[pl.BlockSpec((m, d), lambda j: (0, 0)),
                  pl.BlockSpec((d, d), lambda j: (0, j)),
                  pl.BlockSpec((1, d), lambda j: (0, j))],
        out_specs=pl.BlockSpec((1, m, d), lambda j: (j, 0, 0)),
        out_shape=jax.ShapeDtypeStruct((n_chunks, m, d), F32),
        compiler_params=_params(("arbitrary",)),
        name="adaln",
    )(c_all, w_ada, b_ada.reshape(1, -1))


def _proj_kernel(x_ref, sc_ref, sh_ref, w_ref, bf_ref,
                 qa_ref, ka_ref, va_ref, qb_ref, kb_ref, vb_ref, ga_ref, gb_ref,
                 logf_ref, cum_ref, cumt_ref, *rest, group, carry_cum, emit_bf16):
    if emit_bf16:
        kab_ref, vab_ref, kbb_ref, vbb_ref, carry_ref = rest
    else:
        (carry_ref,) = rest
    nb, r, d = x_ref.shape
    rows = nb * r
    h = x_ref[...] * (1.0 + sc_ref[...]) + sh_ref[...]
    hb = h.reshape(rows, d).astype(BF16)

    def col(start, width):
        return jnp.dot(hb, w_ref[:, start:start + width], preferred_element_type=F32)

    scale = HEAD_DIM ** -0.5
    w = W_ATT
    qa_ref[...] = (col(0, w) * scale).astype(BF16).reshape(nb, r, w)
    ka = col(w, w)
    ka_ref[...] = ka.reshape(nb, r, w)
    va = col(2 * w, w)
    va_ref[...] = va.reshape(nb, r, w)
    qb_ref[...] = (col(3 * w, w) * scale).astype(BF16).reshape(nb, r, w)
    kb = col(4 * w, w)
    kb_ref[...] = kb.reshape(nb, r, w)
    vb = col(5 * w, w)
    vb_ref[...] = vb.reshape(nb, r, w)
    if emit_bf16:
        kab_ref[...] = ka.astype(BF16).reshape(nb, r, w)
        vab_ref[...] = va.astype(BF16).reshape(nb, r, w)
        kbb_ref[...] = kb.astype(BF16).reshape(nb, r, w)
        vbb_ref[...] = vb.astype(BF16).reshape(nb, r, w)
    ga_ref[...] = col(6 * w, d).reshape(nb, r, d)
    gb_ref[...] = col(6 * w + d, d).reshape(nb, r, d)

    fa = col(6 * w + 2 * d, LANES) + bf_ref[...]
    logf = jax.nn.log_sigmoid(fa)
    ri = lax.broadcasted_iota(jnp.int32, (rows, rows), 0)
    ci = lax.broadcasted_iota(jnp.int32, (rows, rows), 1)
    gs = _log2(group)
    tri = jnp.where((ci <= ri) & ((ri >> gs) == (ci >> gs)), 1.0, 0.0).astype(F32)
    cum = jnp.dot(tri, logf, precision=_HI, preferred_element_type=F32)
    if carry_cum:
        @pl.when(pl.program_id(1) == 0)
        def _():
            carry_ref[...] = jnp.zeros_like(carry_ref)
        cum = cum + carry_ref[...]
        carry_ref[...] = cum[rows - 1:rows, :]
    logf_ref[...] = logf[:, :H_FOX].reshape(nb, r, H_FOX)
    cum_ref[...] = cum[:, :H_FOX].reshape(nb, r, H_FOX)
    sel = jnp.where(lax.broadcasted_iota(jnp.int32, (SUBLANES, LANES), 0)
                    == lax.broadcasted_iota(jnp.int32, (SUBLANES, LANES), 1), 1.0, 0.0).astype(F32)
    cumt_ref[0] = lax.dot_general(sel, cum, _NT, precision=_HI, preferred_element_type=F32)


def _project(x, sc, sh, w_in_r, bf_pad, *, seqs_per_block, rows_per_seq_block, group, carry_cum,
             emit_bf16):
    n_seq, t, d = x.shape
    nb, r = seqs_per_block, rows_per_seq_block
    rows = nb * r
    grid = (n_seq // nb, t // r)
    w = W_ATT

    def blk(width):
        return pl.BlockSpec((nb, r, width), lambda i, j: (i, j, 0))

    mod = pl.BlockSpec((nb, 1, d), lambda i, j: (i, 0, 0))
    n_steps = grid[0] * grid[1]
    out_shape = [jax.ShapeDtypeStruct((n_seq, t, w), BF16),
                 jax.ShapeDtypeStruct((n_seq, t, w), F32),
                 jax.ShapeDtypeStruct((n_seq, t, w), F32),
                 jax.ShapeDtypeStruct((n_seq, t, w), BF16),
                 jax.ShapeDtypeStruct((n_seq, t, w), F32),
                 jax.ShapeDtypeStruct((n_seq, t, w), F32),
                 jax.ShapeDtypeStruct((n_seq, t, d), F32),
                 jax.ShapeDtypeStruct((n_seq, t, d), F32),
                 jax.ShapeDtypeStruct((n_seq, t, H_FOX), F32),
                 jax.ShapeDtypeStruct((n_seq, t, H_FOX), F32),
                 jax.ShapeDtypeStruct((n_steps, SUBLANES, rows), F32)]
    out_specs = [blk(w), blk(w), blk(w), blk(w), blk(w), blk(w), blk(d), blk(d),
                 blk(H_FOX), blk(H_FOX),
                 pl.BlockSpec((1, SUBLANES, rows), lambda i, j: (i * grid[1] + j, 0, 0))]
    if emit_bf16:
        out_shape += [jax.ShapeDtypeStruct((n_seq, t, w), BF16)] * 4
        out_specs += [blk(w)] * 4
    return pl.pallas_call(
        functools.partial(_proj_kernel, group=group, carry_cum=carry_cum, emit_bf16=emit_bf16),
        grid=grid,
        in_specs=[blk(d), mod, mod, _resident(w_in_r.shape), _resident(bf_pad.shape)],
        out_specs=out_specs,
        out_shape=out_shape,
        scratch_shapes=[pltpu.VMEM((1, LANES), F32)],
        compiler_params=_params(("arbitrary", "arbitrary")),
        name="project",
    )(x, sc, sh, w_in_r, bf_pad)


def _online_update(s, m, l, acc, v):
    m_new = jnp.maximum(m, jnp.max(s, axis=-1, keepdims=True))
    alpha = jnp.exp(m - m_new)
    p = jnp.exp(s - m_new)
    l_new = alpha * l + jnp.sum(p, axis=-1, keepdims=True)
    acc_new = alpha * acc + jnp.dot(p.astype(BF16), v, preferred_element_type=F32)
    return m_new, l_new, acc_new


def _softmax_init(rows, width):
    return (jnp.full((rows, 1), NEG_INF, F32), jnp.zeros((rows, 1), F32), jnp.zeros((rows, width), F32))


def _fox_prompt_kernel(q_ref, k_ref, v_ref, cq_ref, ck_ref, o_ref, *, blk, heads_per_loop):
    i = pl.program_id(1)
    lane = lax.broadcasted_iota(jnp.int32, (blk, LANES), 1)
    lo = lane < HEAD_DIM
    ri = lax.broadcasted_iota(jnp.int32, (blk, blk), 0)
    ci = lax.broadcasted_iota(jnp.int32, (blk, blk), 1)
    causal = ci <= ri

    def pair_cols(hd):
        return slice((hd // 2) * LANES, (hd // 2 + 1) * LANES)

    def head_query(hd):
        q2 = q_ref[0, :, pair_cols(hd)].astype(F32)
        keep = lo if hd % 2 == 0 else jnp.logical_not(lo)
        return jnp.where(keep, q2, 0.0).astype(BF16)

    for first in range(0, H_FOX, heads_per_loop):
        heads = list(range(first, first + heads_per_loop))
        qs = [head_query(hd) for hd in heads]
        cqs = [cq_ref[0, :, hd:hd + 1] for hd in heads]

        def step(j, carry, masked, heads=heads, qs=qs, cqs=cqs):
            start = pl.multiple_of(j * blk, blk)
            out = []
            for hd, qh, cq, (m, l, acc) in zip(heads, qs, cqs, carry):
                ks = k_ref[0, pl.ds(start, blk), pair_cols(hd)]
                vs = v_ref[0, pl.ds(start, blk), pair_cols(hd)]
                s = lax.dot_general(qh, ks, _NT, preferred_element_type=F32)
                s = s + (cq - ck_ref[0, hd, pl.ds(j, 1), :])
                if masked:
                    s = jnp.where(causal, s, NEG_INF)
                out.append(_online_update(s, m, l, acc, vs))
            return tuple(out)

        init = tuple(_softmax_init(blk, LANES) for _ in heads)
        carry = lax.fori_loop(0, i, functools.partial(step, masked=False), init)
        final = step(i, carry, True)
        for k in range(0, heads_per_loop, 2):
            (_, l0, a0), (_, l1, a1) = final[k], final[k + 1]
            o_ref[0, :, pair_cols(first + k)] = jnp.where(lo, a0 / l0, a1 / l1).astype(BF16)


def _fox_prompt(q, k, v, cum, cum_t, *, blk):
    b, t, w = q.shape
    nblk = t // blk
    return pl.pallas_call(
        functools.partial(_fox_prompt_kernel, blk=blk, heads_per_loop=ATT_HEADS_PER_LOOP),
        grid=(b, nblk),
        in_specs=[pl.BlockSpec((1, blk, w), lambda bi, i: (bi, i, 0)),
                  pl.BlockSpec((1, t, w), lambda bi, i: (bi, 0, 0)),
                  pl.BlockSpec((1, t, w), lambda bi, i: (bi, 0, 0)),
                  pl.BlockSpec((1, blk, H_FOX), lambda bi, i: (bi, i, 0)),
                  pl.BlockSpec((1, H_FOX, nblk, blk), lambda bi, i: (bi, 0, 0, 0))],
        out_specs=pl.BlockSpec((1, blk, w), lambda bi, i: (bi, i, 0)),
        out_shape=jax.ShapeDtypeStruct((b, t, w), BF16),
        compiler_params=_params(("arbitrary", "arbitrary")),
        name="fox_prompt",
    )(q, k, v, cum, cum_t)


def _diff_lambda(lam_ref, lambda_init):
    lv = lam_ref[...]
    s1 = jnp.sum(lv[0:1] * lv[1:2], axis=-1, keepdims=True)
    s2 = jnp.sum(lv[2:3] * lv[3:4], axis=-1, keepdims=True)
    return jnp.exp(s1) - jnp.exp(s2) + lambda_init


def _subln(o, gain, lambda_init):
    ms = jnp.mean(o * o, axis=-1, keepdims=True)
    return o * lax.rsqrt(ms + LN_EPS) * gain * (1.0 - lambda_init)


def _alibi_slope(hd):
    return 2.0 ** (-8.0 * (hd + 1) / H_DIFF)


def _diff_prompt_kernel(q_ref, k_ref, v_ref, lam_ref, gain_ref, o_ref, *, blk, lambda_init,
                        heads_per_loop):
    i = pl.program_id(1)
    lane = lax.broadcasted_iota(jnp.int32, (blk, LANES), 1)
    lo = lane < HEAD_DIM
    ri = lax.broadcasted_iota(jnp.int32, (blk, blk), 0)
    ci = lax.broadcasted_iota(jnp.int32, (blk, blk), 1)
    causal = ci <= ri
    rel = (ci - ri).astype(F32)
    lam = _diff_lambda(lam_ref, lambda_init)

    for first in range(0, H_DIFF, heads_per_loop):
        heads = list(range(first, first + heads_per_loop))
        qmaps = []
        for hd in heads:
            q2 = q_ref[0, :, hd * LANES:(hd + 1) * LANES].astype(F32)
            qmaps.append((jnp.where(lo, q2, 0.0).astype(BF16), jnp.where(lo, 0.0, q2).astype(BF16)))

        def step(j, carry, masked, heads=heads, qmaps=qmaps):
            start = pl.multiple_of(j * blk, blk)
            out = []
            for hd, (q_1, q_2), (c1, c2) in zip(heads, qmaps, carry):
                cols = slice(hd * LANES, (hd + 1) * LANES)
                slope = _alibi_slope(hd)
                ks = k_ref[0, pl.ds(start, blk), cols]
                vs = v_ref[0, pl.ds(start, blk), cols]
                bias = slope * rel + (slope * blk) * (j - i).astype(F32)
                s1 = lax.dot_general(q_1, ks, _NT, preferred_element_type=F32) + bias
                s2 = lax.dot_general(q_2, ks, _NT, preferred_element_type=F32) + bias
                if masked:
                    s1 = jnp.where(causal, s1, NEG_INF)
                    s2 = jnp.where(causal, s2, NEG_INF)
                out.append((_online_update(s1, *c1, vs), _online_update(s2, *c2, vs)))
            return tuple(out)

        init = tuple((_softmax_init(blk, LANES), _softmax_init(blk, LANES)) for _ in heads)
        carry = lax.fori_loop(0, i, functools.partial(step, masked=False), init)
        final = step(i, carry, True)
        for hd, ((_, l1, a1), (_, l2, a2)) in zip(heads, final):
            o = a1 / l1 - lam * (a2 / l2)
            o_ref[0, :, hd * LANES:(hd + 1) * LANES] = _subln(o, gain_ref[...], lambda_init).astype(BF16)


def _diff_prompt(q, k, v, lam_vecs, gain, *, blk, lambda_init):
    b, t, w = q.shape
    return pl.pallas_call(
        functools.partial(_diff_prompt_kernel, blk=blk, lambda_init=lambda_init,
                          heads_per_loop=ATT_HEADS_PER_LOOP // 2),
        grid=(b, t // blk),
        in_specs=[pl.BlockSpec((1, blk, w), lambda bi, i: (bi, i, 0)),
                  pl.BlockSpec((1, t, w), lambda bi, i: (bi, 0, 0)),
                  pl.BlockSpec((1, t, w), lambda bi, i: (bi, 0, 0)),
                  pl.BlockSpec(lam_vecs.shape, lambda bi, i: (0, 0)),
                  pl.BlockSpec(gain.shape, lambda bi, i: (0, 0))],
        out_specs=pl.BlockSpec((1, blk, w), lambda bi, i: (bi, i, 0)),
        out_shape=jax.ShapeDtypeStruct((b, t, w), BF16),
        compiler_params=_params(("arbitrary", "arbitrary")),
        name="diff_prompt",
    )(q, k, v, lam_vecs, gain)


def _suffix_weights(page):
    t_in = np.arange(page)[:, None]
    t_out = (np.arange(page * H_FOX) // H_FOX)[None, :]
    suffix = t_in > t_out
    return np.concatenate([suffix, np.ones_like(suffix)], axis=1).astype(np.float32)


def _suffix_kernel(pt_ref, *refs, n_pages):
    page_refs = refs[:n_pages]
    w_ref, out_ref = refs[n_pages:]
    width = out_ref.shape[-1]
    lrow = jnp.concatenate([pr[...].T for pr in page_refs], axis=0)
    wm = w_ref[...]
    y = sum(jnp.dot(piece, wm, preferred_element_type=F32) for piece in _split3(lrow))
    r = lax.broadcasted_iota(jnp.int32, y.shape, 0)
    c = lax.broadcasted_iota(jnp.int32, y.shape, 1)
    y = jnp.where((c & (H_FOX - 1)) == (r & (H_FOX - 1)), y, 0.0)
    red = y.reshape(n_pages, H_FOX, 2 * width).sum(axis=1)
    off = jnp.zeros((1, width), F32)
    for p in reversed(range(n_pages)):
        out_ref[0, p:p + 1, :] = red[p:p + 1, :width] + off
        off = off + red[p:p + 1, width:]


def _suffix_sums(cache_logf, layer, page_table):
    _, n_phys, page, hf = cache_logf.shape
    n_seq, n_pages = page_table.shape
    width = page * hf
    wmat = jnp.asarray(_suffix_weights(page), dtype=BF16)
    page_specs = [pl.BlockSpec((None, None, page, hf),
                               functools.partial(lambda b, pt, p: (layer, pt[b, p], 0, 0), p=p))
                  for p in range(n_pages)]
    grid_spec = pltpu.PrefetchScalarGridSpec(
        num_scalar_prefetch=1,
        grid=(n_seq,),
        in_specs=page_specs + [pl.BlockSpec(wmat.shape, lambda b, pt: (0, 0))],
        out_specs=pl.BlockSpec((1, n_pages, width), lambda b, pt: (b, 0, 0)),
    )
    return pl.pallas_call(
        functools.partial(_suffix_kernel, n_pages=n_pages),
        grid_spec=grid_spec,
        out_shape=jax.ShapeDtypeStruct((n_seq, n_pages, width), F32),
        compiler_params=_params(("arbitrary",)),
        name="suffix_sums",
    )(page_table, *([cache_logf] * n_pages), wmat)


def _rows_by_group(x, width, groups):
    return jnp.concatenate([x[:, g * width:(g + 1) * width] for g in groups], axis=0)


def _pad_rows(x, rows):
    return jnp.concatenate([x, jnp.zeros((rows - x.shape[0], x.shape[1]), x.dtype)], axis=0)


def _decode_kernel(pt_ref, *refs, pps, page, past_len, lambda_init):
    kf_refs = refs[0:pps]
    vf_refs = refs[pps:2 * pps]
    kd_refs = refs[2 * pps:3 * pps]
    vd_refs = refs[3 * pps:4 * pps]
    (qa_ref, ka_ref, va_ref, qb_ref, kb_ref, vb_ref, cq_ref, ckn_ref, r_ref, lam_ref, gain_ref,
     oa_ref, ob_ref,
     qf_scr, qd_scr, fmask_scr, dbias_scr,
     mf_scr, lf_scr, af_scr, md_scr, ld_scr, ad_scr) = refs[4 * pps:]
    b = pl.program_id(0)
    j = pl.program_id(1)
    n_steps = pl.num_programs(1)
    nq = qa_ref.shape[1]
    rows = H_FOX * nq
    fkeys = pps * page * H_FOX
    dkeys = pps * page * H_DIFF
    qbits = _log2(nq)
    diff_groups = [2 * hd + mp for mp in range(2) for hd in range(H_DIFF)]

    def row_ids(width):
        return lax.broadcasted_iota(jnp.int32, (rows, width), 0)

    def slope_of(row):
        hd = (row >> qbits) & (H_DIFF - 1)
        slope = jnp.zeros(row.shape, F32)
        for k in range(H_DIFF):
            slope = jnp.where(hd == k, _alibi_slope(k), slope)
        return slope

    @pl.when((b == 0) & (j == 0))
    def _():
        row, lane = row_ids(fkeys), lax.broadcasted_iota(jnp.int32, (rows, fkeys), 1)
        fmask_scr[...] = jnp.where((lane & (H_FOX - 1)) == (row >> qbits), 0.0, NEG_INF)
        row, lane = row_ids(dkeys), lax.broadcasted_iota(jnp.int32, (rows, dkeys), 1)
        same_head = (lane & (H_DIFF - 1)) == ((row >> qbits) & (H_DIFF - 1))
        rel = ((lane >> _log2(H_DIFF)) - (row & (nq - 1))).astype(F32) - past_len
        dbias_scr[...] = jnp.where(same_head, slope_of(row) * rel, NEG_INF)

    @pl.when(j == 0)
    def _():
        qf_scr[...] = _rows_by_group(qa_ref[0].astype(F32), HEAD_DIM, range(H_FOX)).astype(BF16)
        qd_scr[...] = _rows_by_group(qb_ref[0].astype(F32), HEAD_DIM, diff_groups).astype(BF16)
        for m_scr, l_scr, a_scr in ((mf_scr, lf_scr, af_scr), (md_scr, ld_scr, ad_scr)):
            m_scr[...] = jnp.full(m_scr.shape, NEG_INF, F32)
            l_scr[...] = jnp.zeros(l_scr.shape, F32)
            a_scr[...] = jnp.zeros(a_scr.shape, F32)

    def update(s, v, m_scr, l_scr, a_scr):
        m, l, acc = _online_update(s, m_scr[...], l_scr[...], a_scr[...], v)
        m_scr[...] = m
        l_scr[...] = l
        a_scr[...] = acc

    def key_rows(page_refs, load):
        return jnp.concatenate([load(pr).reshape(-1, pr.shape[-1]) for pr in page_refs],
                               axis=0).astype(BF16)

    k2 = key_rows(kf_refs, lambda pr: pr[...])
    s_f = lax.dot_general(qf_scr[...], k2, _NT, preferred_element_type=F32)
    r_row = jnp.concatenate([r_ref[0, j, i:i + 1, :] for i in range(pps)], axis=1)
    s_f = s_f + (cq_ref[0] + r_row) + fmask_scr[...]
    update(s_f, key_rows(vf_refs, lambda pr: pr[...]), mf_scr, lf_scr, af_scr)

    qd = qd_scr[...]
    half = rows // 2
    s_d = jnp.concatenate(
        [lax.dot_general(qd[mp * half:(mp + 1) * half],
                         key_rows(kd_refs, lambda pr, mp=mp: pr[pl.ds(mp, page * H_DIFF, stride=2), :]),
                         _NT, preferred_element_type=F32) for mp in range(2)], axis=0)
    step_shift = slope_of(row_ids(1)) * (j * (pps * page)).astype(F32)
    s_d = s_d + dbias_scr[...] + step_shift
    update(s_d, key_rows(vd_refs, lambda pr: pr[...]), md_scr, ld_scr, ad_scr)

    @pl.when(j == n_steps - 1)
    def _():
        row = row_ids(LANES)
        col = lax.broadcasted_iota(jnp.int32, (rows, LANES), 1)
        q_row, s_col = row & (nq - 1), col & (nq - 1)
        causal = s_col <= q_row

        kn = _pad_rows(_rows_by_group(ka_ref[0], HEAD_DIM, range(H_FOX)), LANES).astype(BF16)
        vn = _pad_rows(_rows_by_group(va_ref[0], HEAD_DIM, range(H_FOX)), LANES).astype(BF16)
        ok = jnp.where(col < H_FOX * nq, jnp.where((col >> qbits) == (row >> qbits),
                                                   jnp.where(causal, 1.0, 0.0), 0.0), 0.0) > 0.5
        s_n = lax.dot_general(qf_scr[...], kn, _NT, preferred_element_type=F32)
        s_n = jnp.where(ok, s_n + (cq_ref[0] - ckn_ref[0]), NEG_INF)
        update(s_n, vn, mf_scr, lf_scr, af_scr)

        vn = _pad_rows(_rows_by_group(vb_ref[0], V_DIFF, range(H_DIFF)), LANES).astype(BF16)
        s_n = jnp.concatenate(
            [lax.dot_general(
                qd[mp * half:(mp + 1) * half],
                _pad_rows(_rows_by_group(kb_ref[0], HEAD_DIM, [2 * hd + mp for hd in range(H_DIFF)]),
                          LANES).astype(BF16), _NT, preferred_element_type=F32)
             for mp in range(2)], axis=0)
        ok = jnp.where(col < H_DIFF * nq,
                       jnp.where((col >> qbits) == ((row >> qbits) & (H_DIFF - 1)),
                                 jnp.where(causal, 1.0, 0.0), 0.0), 0.0) > 0.5
        s_n = jnp.where(ok, s_n + slope_of(row) * (s_col - q_row).astype(F32), NEG_INF)
        update(s_n, vn, md_scr, ld_scr, ad_scr)

        of = af_scr[...] / lf_scr[...]
        oa_ref[0] = jnp.concatenate([of[hd * nq:(hd + 1) * nq] for hd in range(H_FOX)],
                                    axis=1).astype(BF16)
        od = ad_scr[...] / ld_scr[...]
        lam = _diff_lambda(lam_ref, lambda_init)
        outs = []
        for hd in range(H_DIFF):
            o = od[hd * nq:(hd + 1) * nq] - lam * od[half + hd * nq:half + (hd + 1) * nq]
            outs.append(_subln(o, gain_ref[...], lambda_init))
        ob_ref[0] = jnp.concatenate(outs, axis=1).astype(BF16)


def _decode_attention(page_table, caches, layer, qa, ka, va, qb, kb, vb, cq_col, ck_new, r_past,
                      lam_vecs, gain, *, lambda_init):
    n_seq, nq, w = qa.shape
    n_pages = page_table.shape[1]
    pps = PAGES_PER_STEP
    page = caches[0].shape[2]
    rows = H_FOX * nq
    caches = [c.reshape(c.shape[:2] + (-1, c.shape[-1])) for c in caches]

    def page_spec(cache, i):
        tail = cache.shape[2:]
        return pl.BlockSpec(
            (None, None) + tail,
            functools.partial(lambda b, j, pt, i: (layer, pt[b, j * pps + i]) + (0,) * len(tail), i=i))

    def per_seq(shape):
        nd = len(shape)
        return pl.BlockSpec((1,) + tuple(shape[1:]), lambda b, j, pt: (b,) + (0,) * (nd - 1))

    cache_specs = [page_spec(c, i) for c in caches for i in range(pps)]
    cache_args = [c for c in caches for _ in range(pps)]
    r_steps = r_past.reshape(n_seq, n_pages // pps, pps, r_past.shape[-1])
    small = [qa, ka, va, qb, kb, vb, cq_col, ck_new, r_steps]
    grid_spec = pltpu.PrefetchScalarGridSpec(
        num_scalar_prefetch=1,
        grid=(n_seq, n_pages // pps),
        in_specs=cache_specs + [per_seq(a.shape) for a in small]
        + [pl.BlockSpec(lam_vecs.shape, lambda b, j, pt: (0, 0)),
           pl.BlockSpec(gain.shape, lambda b, j, pt: (0, 0))],
        out_specs=[per_seq(qa.shape), per_seq(qa.shape)],
        scratch_shapes=[pltpu.VMEM((rows, HEAD_DIM), BF16), pltpu.VMEM((rows, HEAD_DIM), BF16),
                        pltpu.VMEM((rows, pps * page * H_FOX), F32),
                        pltpu.VMEM((rows, pps * page * H_DIFF), F32),
                        pltpu.VMEM((rows, 1), F32), pltpu.VMEM((rows, 1), F32),
                        pltpu.VMEM((rows, HEAD_DIM), F32),
                        pltpu.VMEM((rows, 1), F32), pltpu.VMEM((rows, 1), F32),
                        pltpu.VMEM((rows, V_DIFF), F32)],
    )
    return pl.pallas_call(
        functools.partial(_decode_kernel, pps=pps, page=page, past_len=float(n_pages * page),
                          lambda_init=lambda_init),
        grid_spec=grid_spec,
        out_shape=[jax.ShapeDtypeStruct(qa.shape, BF16)] * 2,
        compiler_params=_params(("arbitrary", "arbitrary")),
        name="decode_attention",
    )(page_table, *cache_args, *small, lam_vecs, gain)


def _layer_norm(x, g, b):
    mu = jnp.mean(x, axis=-1, keepdims=True)
    xc = x - mu
    var = jnp.mean(xc * xc, axis=-1, keepdims=True)
    return xc * lax.rsqrt(var + LN_EPS) * g + b


def _ffn_kernel(x_ref, oa_ref, ob_ref, ga_ref, gb_ref, g1_ref, sh2_ref, sc2_ref, g2_ref,
                wa_ref, wb_ref, wo_ref, ln1g_ref, ln1b_ref, wg_ref, wu_ref, wd_ref,
                ln2g_ref, ln2b_ref, y_ref, *, alpha):
    nb, r, d = x_ref.shape
    rows = nb * r

    def flat(ref):
        return ref[...].reshape(rows, ref.shape[-1])

    def per_seq(ref):
        return jnp.broadcast_to(ref[...], (nb, r, d)).reshape(rows, d)

    a = jnp.dot(flat(oa_ref), wa_ref[...], preferred_element_type=F32)
    b = jnp.dot(flat(ob_ref), wb_ref[...], preferred_element_type=F32)
    m = jax.nn.sigmoid(flat(ga_ref)) * a + jax.nn.sigmoid(flat(gb_ref)) * b
    mo = jnp.dot(m.astype(BF16), wo_ref[...], preferred_element_type=F32)
    x1 = _layer_norm(alpha * flat(x_ref) + per_seq(g1_ref) * mo, ln1g_ref[...], ln1b_ref[...])
    h2 = (x1 * (1.0 + per_seq(sc2_ref)) + per_seq(sh2_ref)).astype(BF16)
    g = jnp.dot(h2, wg_ref[...], preferred_element_type=F32)
    u = jnp.dot(h2, wu_ref[...], preferred_element_type=F32)
    act = (g * jax.nn.sigmoid(g) * u).astype(BF16)
    f = jnp.dot(act, wd_ref[...], preferred_element_type=F32)
    y = _layer_norm(alpha * x1 + per_seq(g2_ref) * f, ln2g_ref[...], ln2b_ref[...])
    y_ref[...] = y.reshape(nb, r, d)


def _merge_ffn(x, oa, ob, ga, gb, g1, sh2, sc2, g2, weights, *, seqs_per_block, rows_per_seq_block,
               alpha):
    n_seq, t, d = x.shape
    nb, r = seqs_per_block, rows_per_seq_block

    def blk(width):
        return pl.BlockSpec((nb, r, width), lambda i, j: (i, j, 0))

    mod = pl.BlockSpec((nb, 1, d), lambda i, j: (i, 0, 0))
    return pl.pallas_call(
        functools.partial(_ffn_kernel, alpha=alpha),
        grid=(n_seq // nb, t // r),
        in_specs=[blk(d), blk(W_ATT), blk(W_ATT), blk(d), blk(d), mod, mod, mod, mod]
        + [_resident(wt.shape) for wt in weights],
        out_specs=blk(d),
        out_shape=jax.ShapeDtypeStruct((n_seq, t, d), F32),
        compiler_params=_params(("arbitrary", "arbitrary")),
        name="merge_ffn",
    )(x, oa, ob, ga, gb, g1, sh2, sc2, g2, *weights)


def kernel(x_prompt, x_sample, c_prompt, c_sample, cache_k_fox, cache_v_fox, cache_logf_fox,
           cache_k_diff, cache_v_diff, page_table, w_ada, b_ada, w_in, b_forget,
           lambda_q1, lambda_k1, lambda_q2, lambda_k2, subln_gain, w_branch_a, w_branch_b,
           w_out, ln1_gain, ln1_bias, w_ffn_gate, w_ffn_up, w_ffn_down, ln2_gain, ln2_bias):
    depth = w_in.shape[0]
    batch, seq, d = x_prompt.shape
    dec_batch, dec_seq, _ = x_sample.shape
    alpha = (2 * depth) ** 0.25
    w = W_ATT
    f_start = 3 * w
    caches = (cache_k_fox, cache_v_fox, cache_k_diff, cache_v_diff)

    xp, xs = x_prompt, x_sample
    outs_p = [[] for _ in range(5)]
    outs_s = [[] for _ in range(5)]
    for l in range(depth):
        lambda_init = 0.8 - 0.6 * math.exp(-0.3 * l)
        w_in_r = jnp.concatenate(
            [w_in[l][:, :f_start], w_in[l][:, f_start + H_FOX:], w_in[l][:, f_start:f_start + H_FOX],
             jnp.zeros((d, LANES - H_FOX), F32)], axis=1).astype(BF16)
        bf_pad = jnp.concatenate([b_forget[l], jnp.zeros((LANES - H_FOX,), F32)]).reshape(1, LANES)
        lam_vecs = jnp.stack([lambda_q1[l], lambda_k1[l], lambda_q2[l], lambda_k2[l]]).astype(F32)
        gain = subln_gain[l].reshape(1, V_DIFF)
        ffn_w = (w_branch_a[l].astype(BF16), w_branch_b[l].astype(BF16), w_out[l].astype(BF16),
                 ln1_gain[l].reshape(1, d), ln1_bias[l].reshape(1, d),
                 w_ffn_gate[l].astype(BF16), w_ffn_up[l].astype(BF16), w_ffn_down[l].astype(BF16),
                 ln2_gain[l].reshape(1, d), ln2_bias[l].reshape(1, d))

        mod = _adaln(jnp.concatenate([c_prompt, c_sample], axis=0), w_ada[l], b_ada[l])
        mod_p = [mod[k, :batch].reshape(batch, 1, d) for k in range(6)]
        mod_s = [mod[k, batch:].reshape(dec_batch, 1, d) for k in range(6)]

        sh1, sc1, g1, sh2, sc2, g2 = mod_p
        (qa, ka, va, qb, kb, vb, ga, gb, logf, cum, cum_t, ka_b, va_b, kb_b, vb_b) = _project(
            xp, sc1, sh1, w_in_r, bf_pad, seqs_per_block=1, rows_per_seq_block=PROJ_ROWS,
            group=seq, carry_cum=True, emit_bf16=True)
        nblk = seq // ATT_BLOCK
        cum_rows = cum_t.reshape(batch, seq // PROJ_ROWS, SUBLANES, PROJ_ROWS)
        cum_rows = cum_rows.transpose(0, 2, 1, 3).reshape(batch, SUBLANES, nblk, ATT_BLOCK)
        oa = _fox_prompt(qa, ka_b, va_b, cum, cum_rows, blk=ATT_BLOCK)
        ob = _diff_prompt(qb, kb_b, vb_b, lam_vecs, gain, blk=ATT_BLOCK, lambda_init=lambda_init)
        xp = _merge_ffn(xp, oa, ob, ga, gb, g1, sh2, sc2, g2, ffn_w,
                        seqs_per_block=1, rows_per_seq_block=FFN_ROWS, alpha=alpha)
        for lst, val in zip(outs_p, (ka.reshape(batch, seq, H_FOX, HEAD_DIM),
                                     va.reshape(batch, seq, H_FOX, HEAD_DIM), logf,
                                     kb.reshape(batch, seq, H_DIFF, 2, HEAD_DIM),
                                     vb.reshape(batch, seq, H_DIFF, V_DIFF))):
            lst.append(val)

        sh1, sc1, g1, sh2, sc2, g2 = mod_s
        seqs_blk = PROJ_ROWS // dec_seq
        (qa, ka, va, qb, kb, vb, ga, gb, logf, cum, _unused) = _project(
            xs, sc1, sh1, w_in_r, bf_pad, seqs_per_block=seqs_blk, rows_per_seq_block=dec_seq,
            group=dec_seq, carry_cum=False, emit_bf16=False)
        cum_hq = cum.transpose(0, 2, 1)
        cq_col = cum_hq.reshape(dec_batch, H_FOX * dec_seq, 1)
        ck_new = jnp.pad(cum_hq.reshape(dec_batch, 1, H_FOX * dec_seq),
                         ((0, 0), (0, 0), (0, LANES - H_FOX * dec_seq)))
        r_past = _suffix_sums(cache_logf_fox, l, page_table)
        oa, ob = _decode_attention(page_table, caches, l, qa, ka, va, qb, kb, vb, cq_col, ck_new,
                                   r_past, lam_vecs, gain, lambda_init=lambda_init)
        ffn_seqs = FFN_ROWS // dec_seq
        xs = _merge_ffn(xs, oa, ob, ga, gb, g1, sh2, sc2, g2, ffn_w,
                        seqs_per_block=ffn_seqs, rows_per_seq_block=dec_seq, alpha=alpha)
        for lst, val in zip(outs_s, (ka.reshape(dec_batch, dec_seq, H_FOX, HEAD_DIM),
                                     va.reshape(dec_batch, dec_seq, H_FOX, HEAD_DIM), logf,
                                     kb.reshape(dec_batch, dec_seq, H_DIFF, 2, HEAD_DIM),
                                     vb.reshape(dec_batch, dec_seq, H_DIFF, V_DIFF))):
            lst.append(val)

    return (xp, xs, *[jnp.stack(o, 0) for o in outs_p], *[jnp.stack(o, 0) for o in outs_s])
```

```python
import functools
import math

import jax
import jax.numpy as jnp
import numpy as np
from jax import lax
from jax.experimental import pallas as pl
from jax.experimental.pallas import tpu as pltpu

F32 = jnp.float32
BF16 = jnp.bfloat16

LANES = 128
SUBLANES = 8
HEAD_DIM = 64
H_FOX = 8
H_DIFF = 4
V_DIFF = 2 * HEAD_DIM
W_ATT = H_FOX * HEAD_DIM
LN_EPS = 1e-5
NEG_INF = -1e30
VMEM_LIMIT = 56 * 1024 * 1024

PROJ_ROWS = 512
ATT_BLOCK = 256
ATT_HEADS_PER_LOOP = 4
FFN_ROWS = 256
PAGES_PER_STEP = 4

_HI = lax.Precision.HIGHEST
_NT = (((1,), (1,)), ((), ()))


def _log2(n):
    assert n & (n - 1) == 0, n
    return n.bit_length() - 1


def _resident(shape):
    nd = len(shape)
    return pl.BlockSpec(shape, lambda *_: (0,) * nd, pipeline_mode=pl.Buffered(1))


def _params(sem):
    return pltpu.CompilerParams(dimension_semantics=sem, vmem_limit_bytes=VMEM_LIMIT)


def _split3(x):
    hi = x.astype(BF16)
    r1 = x - hi.astype(F32)
    mid = r1.astype(BF16)
    lo = (r1 - mid.astype(F32)).astype(BF16)
    return hi, mid, lo


def _ada_kernel(c_ref, w_ref, b_ref, o_ref):
    o_ref[0] = jnp.dot(c_ref[...].astype(BF16), w_ref[...].astype(BF16),
                       preferred_element_type=F32) + b_ref[...]


def _adaln(c_all, w_ada, b_ada):
    m, d = c_all.shape
    n_chunks = w_ada.shape[1] // d
    return pl.pallas_call(
        _ada_kernel,
        grid=(n_chunks,),
        in_specs=[pl.BlockSpec((m, d), lambda j: (0, 0)),
                  pl.BlockSpec((d, d), lambda j: (0, j)),
                  pl.BlockSpec((1, d), lambda j: (0, j))],
        out_specs=pl.BlockSpec((1, m, d), lambda j: (j, 0, 0)),
        out_shape=jax.ShapeDtypeStruct((n_chunks, m, d), F32),
        compiler_params=_params(("arbitrary",)),
        name="adaln",
    )(c_all, w_ada, b_ada.reshape(1, -1))


def _proj_kernel(x_ref, sc_ref, sh_ref, w_ref, bf_ref,
                 qa_ref, ka_ref, va_ref, qb_ref, kb_ref, vb_ref, ga_ref, gb_ref,
                 logf_ref, cum_ref, cumt_ref, *rest, group, carry_cum, emit_bf16):
    if emit_bf16:
        kab_ref, vab_ref, kbb_ref, vbb_ref, carry_ref = rest
    else:
        (carry_ref,) = rest
    nb, r, d = x_ref.shape
    rows = nb * r
    h = x_ref[...] * (1.0 + sc_ref[...]) + sh_ref[...]
    hb = h.reshape(rows, d).astype(BF16)

    def col(start, width):
        return jnp.dot(hb, w_ref[:, start:start + width], preferred_element_type=F32)

    scale = HEAD_DIM ** -0.5
    w = W_ATT
    qa_ref[...] = (col(0, w) * scale).astype(BF16).reshape(nb, r, w)
    ka = col(w, w)
    ka_ref[...] = ka.reshape(nb, r, w)
    va = col(2 * w, w)
    va_ref[...] = va.reshape(nb, r, w)
    qb_ref[...] = (col(3 * w, w) * scale).astype(BF16).reshape(nb, r, w)
    kb = col(4 * w, w)
    kb_ref[...] = kb.reshape(nb, r, w)
    vb = col(5 * w, w)
    vb_ref[...] = vb.reshape(nb, r, w)
    if emit_bf16:
        kab_ref[...] = ka.astype(BF16).reshape(nb, r, w)
        vab_ref[...] = va.astype(BF16).reshape(nb, r, w)
        kbb_ref[...] = kb.astype(BF16).reshape(nb, r, w)
        vbb_ref[...] = vb.astype(BF16).reshape(nb, r, w)
    ga_ref[...] = col(6 * w, d).reshape(nb, r, d)
    gb_ref[...] = col(6 * w + d, d).reshape(nb, r, d)

    fa = col(6 * w + 2 * d, LANES) + bf_ref[...]
    logf = jax.nn.log_sigmoid(fa)
    ri = lax.broadcasted_iota(jnp.int32, (rows, rows), 0)
    ci = lax.broadcasted_iota(jnp.int32, (rows, rows), 1)
    gs = _log2(group)
    tri = jnp.where((ci <= ri) & ((ri >> gs) == (ci >> gs)), 1.0, 0.0).astype(F32)
    cum = jnp.dot(tri, logf, precision=_HI, preferred_element_type=F32)
    if carry_cum:
        @pl.when(pl.program_id(1) == 0)
        def _():
            carry_ref[...] = jnp.zeros_like(carry_ref)
        cum = cum + carry_ref[...]
        carry_ref[...] = cum[rows - 1:rows, :]
    logf_ref[...] = logf[:, :H_FOX].reshape(nb, r, H_FOX)
    cum_ref[...] = cum[:, :H_FOX].reshape(nb, r, H_FOX)
    sel = jnp.where(lax.broadcasted_iota(jnp.int32, (SUBLANES, LANES), 0)
                    == lax.broadcasted_iota(jnp.int32, (SUBLANES, LANES), 1), 1.0, 0.0).astype(F32)
    cumt_ref[0] = lax.dot_general(sel, cum, _NT, precision=_HI, preferred_element_type=F32)


def _project(x, sc, sh, w_in_r, bf_pad, *, seqs_per_block, rows_per_seq_block, group, carry_cum,
             emit_bf16):
    n_seq, t, d = x.shape
    nb, r = seqs_per_block, rows_per_seq_block
    rows = nb * r
    grid = (n_seq // nb, t // r)
    w = W_ATT

    def blk(width):
        return pl.BlockSpec((nb, r, width), lambda i, j: (i, j, 0))

    mod = pl.BlockSpec((nb, 1, d), lambda i, j: (i, 0, 0))
    n_steps = grid[0] * grid[1]
    out_shape = [jax.ShapeDtypeStruct((n_seq, t, w), BF16),
                 jax.ShapeDtypeStruct((n_seq, t, w), F32),
                 jax.ShapeDtypeStruct((n_seq, t, w), F32),
                 jax.ShapeDtypeStruct((n_seq, t, w), BF16),
                 jax.ShapeDtypeStruct((n_seq, t, w), F32),
                 jax.ShapeDtypeStruct((n_seq, t, w), F32),
                 jax.ShapeDtypeStruct((n_seq, t, d), F32),
                 jax.ShapeDtypeStruct((n_seq, t, d), F32),
                 jax.ShapeDtypeStruct((n_seq, t, H_FOX), F32),
                 jax.ShapeDtypeStruct((n_seq, t, H_FOX), F32),
                 jax.ShapeDtypeStruct((n_steps, SUBLANES, rows), F32)]
    out_specs = [blk(w), blk(w), blk(w), blk(w), blk(w), blk(w), blk(d), blk(d),
                 blk(H_FOX), blk(H_FOX),
                 pl.BlockSpec((1, SUBLANES, rows), lambda i, j: (i * grid[1] + j, 0, 0))]
    if emit_bf16:
        out_shape += [jax.ShapeDtypeStruct((n_seq, t, w), BF16)] * 4
        out_specs += [blk(w)] * 4
    return pl.pallas_call(
        functools.partial(_proj_kernel, group=group, carry_cum=carry_cum, emit_bf16=emit_bf16),
        grid=grid,
        in_specs=[blk(d), mod, mod, _resident(w_in_r.shape), _resident(bf_pad.shape)],
        out_specs=out_specs,
        out_shape=out_shape,
        scratch_shapes=[pltpu.VMEM((1, LANES), F32)],
        compiler_params=_params(("arbitrary", "arbitrary")),
        name="project",
    )(x, sc, sh, w_in_r, bf_pad)


def _online_update(s, m, l, acc, v):
    m_new = jnp.maximum(m, jnp.max(s, axis=-1, keepdims=True))
    alpha = jnp.exp(m - m_new)
    p = jnp.exp(s - m_new)
    l_new = alpha * l + jnp.sum(p, axis=-1, keepdims=True)
    acc_new = alpha * acc + jnp.dot(p.astype(BF16), v, preferred_element_type=F32)
    return m_new, l_new, acc_new


def _softmax_init(rows, width):
    return (jnp.full((rows, 1), NEG_INF, F32), jnp.zeros((rows, 1), F32), jnp.zeros((rows, width), F32))


def _fox_prompt_kernel(q_ref, k_ref, v_ref, cq_ref, ck_ref, o_ref, *, blk, heads_per_loop):
    i = pl.program_id(1)
    lane = lax.broadcasted_iota(jnp.int32, (blk, LANES), 1)
    lo = lane < HEAD_DIM
    ri = lax.broadcasted_iota(jnp.int32, (blk, blk), 0)
    ci = lax.broadcasted_iota(jnp.int32, (blk, blk), 1)
    causal = ci <= ri

    def pair_cols(hd):
        return slice((hd // 2) * LANES, (hd // 2 + 1) * LANES)

    def head_query(hd):
        q2 = q_ref[0, :, pair_cols(hd)].astype(F32)
        keep = lo if hd % 2 == 0 else jnp.logical_not(lo)
        return jnp.where(keep, q2, 0.0).astype(BF16)

    for first in range(0, H_FOX, heads_per_loop):
        heads = list(range(first, first + heads_per_loop))
        qs = [head_query(hd) for hd in heads]
        cqs = [cq_ref[0, :, hd:hd + 1] for hd in heads]

        def step(j, carry, masked, heads=heads, qs=qs, cqs=cqs):
            start = pl.multiple_of(j * blk, blk)
            out = []
            for hd, qh, cq, (m, l, acc) in zip(heads, qs, cqs, carry):
                ks = k_ref[0, pl.ds(start, blk), pair_cols(hd)]
                vs = v_ref[0, pl.ds(start, blk), pair_cols(hd)]
                s = lax.dot_general(qh, ks, _NT, preferred_element_type=F32)
                s = s + (cq - ck_ref[0, hd, pl.ds(j, 1), :])
                if masked:
                    s = jnp.where(causal, s, NEG_INF)
                out.append(_online_update(s, m, l, acc, vs))
            return tuple(out)

        init = tuple(_softmax_init(blk, LANES) for _ in heads)
        carry = lax.fori_loop(0, i, functools.partial(step, masked=False), init)
        final = step(i, carry, True)
        for k in range(0, heads_per_loop, 2):
            (_, l0, a0), (_, l1, a1) = final[k], final[k + 1]
            o_ref[0, :, pair_cols(first + k)] = jnp.where(lo, a0 / l0, a1 / l1).astype(BF16)


def _fox_prompt(q, k, v, cum, cum_t, *, blk):
    b, t, w = q.shape
    nblk = t // blk
    return pl.pallas_call(
        functools.partial(_fox_prompt_kernel, blk=blk, heads_per_loop=ATT_HEADS_PER_LOOP),
        grid=(b, nblk),
        in_specs=[pl.BlockSpec((1, blk, w), lambda bi, i: (bi, i, 0)),
                  pl.BlockSpec((1, t, w), lambda bi, i: (bi, 0, 0)),
                  pl.BlockSpec((1, t, w), lambda bi, i: (bi, 0, 0)),
                  pl.BlockSpec((1, blk, H_FOX), lambda bi, i: (bi, i, 0)),
                  pl.BlockSpec((1, H_FOX, nblk, blk), lambda bi, i: (bi, 0, 0, 0))],
        out_specs=pl.BlockSpec((1, blk, w), lambda bi, i: (bi, i, 0)),
        out_shape=jax.ShapeDtypeStruct((b, t, w), BF16),
        compiler_params=_params(("arbitrary", "arbitrary")),
        name="fox_prompt",
    )(q, k, v, cum, cum_t)


def _diff_lambda(lam_ref, lambda_init):
    lv = lam_ref[...]
    s1 = jnp.sum(lv[0:1] * lv[1:2], axis=-1, keepdims=True)
    s2 = jnp.sum(lv[2:3] * lv[3:4], axis=-1, keepdims=True)
    return jnp.exp(s1) - jnp.exp(s2) + lambda_init


def _subln(o, gain, lambda_init):
    ms = jnp.mean(o * o, axis=-1, keepdims=True)
    return o * lax.rsqrt(ms + LN_EPS) * gain * (1.0 - lambda_init)


def _alibi_slope(hd):
    return 2.0 ** (-8.0 * (hd + 1) / H_DIFF)


def _diff_prompt_kernel(q_ref, k_ref, v_ref, lam_ref, gain_ref, o_ref, *, blk, lambda_init,
                        heads_per_loop):
    i = pl.program_id(1)
    lane = lax.broadcasted_iota(jnp.int32, (blk, LANES), 1)
    lo = lane < HEAD_DIM
    ri = lax.broadcasted_iota(jnp.int32, (blk, blk), 0)
    ci = lax.broadcasted_iota(jnp.int32, (blk, blk), 1)
    causal = ci <= ri
    rel = (ci - ri).astype(F32)
    lam = _diff_lambda(lam_ref, lambda_init)

    for first in range(0, H_DIFF, heads_per_loop):
        heads = list(range(first, first + heads_per_loop))
        qmaps = []
        for hd in heads:
            q2 = q_ref[0, :, hd * LANES:(hd + 1) * LANES].astype(F32)
            qmaps.append((jnp.where(lo, q2, 0.0).astype(BF16), jnp.where(lo, 0.0, q2).astype(BF16)))

        def step(j, carry, masked, heads=heads, qmaps=qmaps):
            start = pl.multiple_of(j * blk, blk)
            out = []
            for hd, (q_1, q_2), (c1, c2) in zip(heads, qmaps, carry):
                cols = slice(hd * LANES, (hd + 1) * LANES)
                slope = _alibi_slope(hd)
                ks = k_ref[0, pl.ds(start, blk), cols]
                vs = v_ref[0, pl.ds(start, blk), cols]
                bias = slope * rel + (slope * blk) * (j - i).astype(F32)
                s1 = lax.dot_general(q_1, ks, _NT, preferred_element_type=F32) + bias
                s2 = lax.dot_general(q_2, ks, _NT, preferred_element_type=F32) + bias
                if masked:
                    s1 = jnp.where(causal, s1, NEG_INF)
                    s2 = jnp.where(causal, s2, NEG_INF)
                out.append((_online_update(s1, *c1, vs), _online_update(s2, *c2, vs)))
            return tuple(out)

        init = tuple((_softmax_init(blk, LANES), _softmax_init(blk, LANES)) for _ in heads)
        carry = lax.fori_loop(0, i, functools.partial(step, masked=False), init)
        final = step(i, carry, True)
        for hd, ((_, l1, a1), (_, l2, a2)) in zip(heads, final):
            o = a1 / l1 - lam * (a2 / l2)
            o_ref[0, :, hd * LANES:(hd + 1) * LANES] = _subln(o, gain_ref[...], lambda_init).astype(BF16)


def _diff_prompt(q, k, v, lam_vecs, gain, *, blk, lambda_init):
    b, t, w = q.shape
    return pl.pallas_call(
        functools.partial(_diff_prompt_kernel, blk=blk, lambda_init=lambda_init,
                          heads_per_loop=ATT_HEADS_PER_LOOP // 2),
        grid=(b, t // blk),
        in_specs=[pl.BlockSpec((1, blk, w), lambda bi, i: (bi, i, 0)),
                  pl.BlockSpec((1, t, w), lambda bi, i: (bi, 0, 0)),
                  pl.BlockSpec((1, t, w), lambda bi, i: (bi, 0, 0)),
                  pl.BlockSpec(lam_vecs.shape, lambda bi, i: (0, 0)),
                  pl.BlockSpec(gain.shape, lambda bi, i: (0, 0))],
        out_specs=pl.BlockSpec((1, blk, w), lambda bi, i: (bi, i, 0)),
        out_shape=jax.ShapeDtypeStruct((b, t, w), BF16),
        compiler_params=_params(("arbitrary", "arbitrary")),
        name="diff_prompt",
    )(q, k, v, lam_vecs, gain)


def _suffix_weights(page):
    t_in = np.arange(page)[:, None]
    suffix = t_in > np.arange(page)[None, :]
    return np.concatenate([suffix, np.ones_like(suffix)], axis=1).astype(np.float32)


def _suffix_kernel(pt_ref, *refs, n_pages):
    page_refs = refs[:n_pages]
    w_ref, out_ref = refs[n_pages:]
    hf, page = page_refs[0].shape
    lrow = jnp.concatenate([pr[...] for pr in page_refs], axis=0)
    wm = w_ref[...]
    y = sum(jnp.dot(piece, wm, preferred_element_type=F32) for piece in _split3(lrow))
    off = jnp.zeros((hf, page), F32)
    for p in reversed(range(n_pages)):
        out_ref[0, p] = y[p * hf:(p + 1) * hf, :page] + off
        off = off + y[p * hf:(p + 1) * hf, page:]


def _suffix_sums(logf_t, layer, page_table):
    _, n_phys, hf, page = logf_t.shape
    n_seq, n_pages = page_table.shape
    wmat = jnp.asarray(_suffix_weights(page), dtype=BF16)
    page_specs = [pl.BlockSpec((None, None, hf, page),
                               functools.partial(lambda b, pt, p: (layer, pt[b, p], 0, 0), p=p))
                  for p in range(n_pages)]
    grid_spec = pltpu.PrefetchScalarGridSpec(
        num_scalar_prefetch=1,
        grid=(n_seq,),
        in_specs=page_specs + [pl.BlockSpec(wmat.shape, lambda b, pt: (0, 0))],
        out_specs=pl.BlockSpec((1, n_pages, hf, page), lambda b, pt: (b, 0, 0, 0)),
    )
    return pl.pallas_call(
        functools.partial(_suffix_kernel, n_pages=n_pages),
        grid_spec=grid_spec,
        out_shape=jax.ShapeDtypeStruct((n_seq, n_pages, hf, page), F32),
        compiler_params=_params(("arbitrary",)),
        name="suffix_sums",
    )(page_table, *([logf_t] * n_pages), wmat)


def _block_diag_queries(q):
    nq, w = q.shape
    groups = w // HEAD_DIM
    tiled = jnp.broadcast_to(q.astype(F32)[None], (groups, nq, w)).reshape(groups * nq, w)
    ri = lax.broadcasted_iota(jnp.int32, (groups * nq, w), 0)
    ci = lax.broadcasted_iota(jnp.int32, (groups * nq, w), 1)
    return jnp.where((ri >> _log2(nq)) == (ci >> _log2(HEAD_DIM)), tiled, 0.0).astype(BF16)


def _online_update_fn(s, m, l, acc, weighted_values):
    m_new = jnp.maximum(m, jnp.max(s, axis=-1, keepdims=True))
    alpha = jnp.exp(m - m_new)
    p = jnp.exp(s - m_new)
    l_new = alpha * l + jnp.sum(p, axis=-1, keepdims=True)
    return m_new, l_new, alpha * acc + weighted_values(p)


def _decode_kernel(pt_ref, *refs, pps, past_len, lambda_init):
    kf_refs = refs[0:pps]
    vf_refs = refs[pps:2 * pps]
    kd_refs = refs[2 * pps:3 * pps]
    vd_refs = refs[3 * pps:4 * pps]
    (qa_ref, ka_ref, va_ref, qb_ref, kb_ref, vb_ref, cq_ref, ckn_ref, r_ref, lam_ref, gain_ref,
     oa_ref, ob_ref,
     qf_scr, qd_scr, mf_scr, lf_scr, af_scr, md_scr, ld_scr, ad_scr) = refs[4 * pps:]
    j = pl.program_id(1)
    n_steps = pl.num_programs(1)
    nq = qa_ref.shape[1]
    page = kf_refs[0].shape[-1]
    rows = H_FOX * nq
    chunk = pps * page

    @pl.when(j == 0)
    def _():
        qf_scr[...] = _block_diag_queries(qa_ref[0])
        qd_scr[...] = _block_diag_queries(qb_ref[0])
        for m_scr, l_scr, a_scr in ((mf_scr, lf_scr, af_scr), (md_scr, ld_scr, ad_scr)):
            m_scr[...] = jnp.full(m_scr.shape, NEG_INF, F32)
            l_scr[...] = jnp.zeros(l_scr.shape, F32)
            a_scr[...] = jnp.zeros(a_scr.shape, F32)

    def update(new_state, m_scr, l_scr, a_scr):
        m_scr[...], l_scr[...], a_scr[...] = new_state

    def state(m_scr, l_scr, a_scr):
        return m_scr[...], l_scr[...], a_scr[...]

    def pages_t(page_refs):
        return jnp.concatenate([pr[...].reshape(W_ATT, page) for pr in page_refs], axis=1).astype(BF16)

    row = lax.broadcasted_iota(jnp.int32, (rows, 1), 0)
    q_of_row = (row & (nq - 1)).astype(F32)
    slope_row = jnp.zeros((rows, 1), F32)
    for hd in range(H_DIFF):
        slope_row = jnp.where((row >> _log2(2 * nq)) == hd, _alibi_slope(hd), slope_row)

    s_f = jnp.dot(qf_scr[...], pages_t(kf_refs), preferred_element_type=F32)
    r_tile = jnp.concatenate(
        [jnp.concatenate([jnp.broadcast_to(r_ref[0, i, hd:hd + 1, :], (nq, page)) for i in range(pps)],
                         axis=1) for hd in range(H_FOX)], axis=0)
    s_f = s_f + (cq_ref[0] + r_tile)
    vf_t = pages_t(vf_refs)

    def fox_values(p):
        return lax.dot_general(p.astype(BF16), vf_t, _NT, preferred_element_type=F32)

    update(_online_update_fn(s_f, *state(mf_scr, lf_scr, af_scr), fox_values), mf_scr, lf_scr, af_scr)

    s_d = jnp.dot(qd_scr[...], pages_t(kd_refs), preferred_element_type=F32)
    key_pos = (lax.broadcasted_iota(jnp.int32, (rows, chunk), 1) + j * chunk).astype(F32)
    s_d = s_d + slope_row * (key_pos - (past_len + q_of_row))

    def diff_values(p, value_rows):
        per_head = 2 * nq
        return jnp.concatenate(
            [jnp.dot(p[hd * per_head:(hd + 1) * per_head].astype(BF16), value_rows(hd),
                     preferred_element_type=F32) for hd in range(H_DIFF)], axis=0)

    def cached_diff_values(hd):
        return jnp.concatenate([pr[pl.ds(hd, page, stride=H_DIFF), :] for pr in vd_refs],
                               axis=0).astype(BF16)

    update(_online_update_fn(s_d, *state(md_scr, ld_scr, ad_scr),
                             lambda p: diff_values(p, cached_diff_values)), md_scr, ld_scr, ad_scr)

    @pl.when(j == n_steps - 1)
    def _():
        pad = jnp.zeros((LANES - nq, W_ATT), F32)

        def padded(ref):
            return jnp.concatenate([ref[0], pad], axis=0).astype(BF16)

        col = lax.broadcasted_iota(jnp.int32, (rows, LANES), 1)
        visible = col <= (row & (nq - 1))
        s_n = lax.dot_general(qf_scr[...], padded(ka_ref), _NT, preferred_element_type=F32)
        s_n = jnp.where(visible, s_n + (cq_ref[0] - ckn_ref[0]), NEG_INF)
        update(_online_update(s_n, *state(mf_scr, lf_scr, af_scr), padded(va_ref)),
               mf_scr, lf_scr, af_scr)
        s_n = lax.dot_general(qd_scr[...], padded(kb_ref), _NT, preferred_element_type=F32)
        s_n = jnp.where(visible, s_n + slope_row * (col.astype(F32) - q_of_row), NEG_INF)
        vb_new = padded(vb_ref)
        update(_online_update_fn(
            s_n, *state(md_scr, ld_scr, ad_scr),
            lambda p: diff_values(p, lambda hd: vb_new[:, hd * V_DIFF:(hd + 1) * V_DIFF])),
            md_scr, ld_scr, ad_scr)

        of = af_scr[...] / lf_scr[...]
        lane = lax.broadcasted_iota(jnp.int32, (nq, W_ATT), 1)
        oa = jnp.zeros((nq, W_ATT), F32)
        for hd in range(H_FOX):
            oa = jnp.where((lane >> _log2(HEAD_DIM)) == hd, of[hd * nq:(hd + 1) * nq, :], oa)
        oa_ref[0] = oa.astype(BF16)

        od = ad_scr[...] / ld_scr[...]
        lam = _diff_lambda(lam_ref, lambda_init)
        outs = []
        for hd in range(H_DIFF):
            r0 = hd * 2 * nq
            o = od[r0:r0 + nq] - lam * od[r0 + nq:r0 + 2 * nq]
            outs.append(_subln(o, gain_ref[...], lambda_init))
        ob_ref[0] = jnp.concatenate(outs, axis=1).astype(BF16)


def _decode_attention(page_table, caches_t, layer, qa, ka, va, qb, kb, vb, cq_col, ck_new, r_past,
                      lam_vecs, gain, *, lambda_init):
    n_seq, nq, w = qa.shape
    n_pages = page_table.shape[1]
    pps = PAGES_PER_STEP
    page = caches_t[0].shape[-1]
    rows = H_FOX * nq

    def page_spec(cache, i):
        tail = cache.shape[2:]
        return pl.BlockSpec(
            (None, None) + tail,
            functools.partial(lambda b, j, pt, i: (layer, pt[b, j * pps + i]) + (0,) * len(tail), i=i))

    def per_seq(shape):
        nd = len(shape)
        return pl.BlockSpec((1,) + tuple(shape[1:]), lambda b, j, pt: (b,) + (0,) * (nd - 1))

    cache_specs = [page_spec(c, i) for c in caches_t for i in range(pps)]
    cache_args = [c for c in caches_t for _ in range(pps)]
    small = [qa, ka, va, qb, kb, vb, cq_col, ck_new]
    r_spec = pl.BlockSpec((1, pps) + r_past.shape[2:], lambda b, j, pt: (b, j, 0, 0))
    grid_spec = pltpu.PrefetchScalarGridSpec(
        num_scalar_prefetch=1,
        grid=(n_seq, n_pages // pps),
        in_specs=cache_specs + [per_seq(a.shape) for a in small] + [r_spec]
        + [pl.BlockSpec(lam_vecs.shape, lambda b, j, pt: (0, 0)),
           pl.BlockSpec(gain.shape, lambda b, j, pt: (0, 0))],
        out_specs=[per_seq(qa.shape), per_seq(qa.shape)],
        scratch_shapes=[pltpu.VMEM((rows, w), BF16), pltpu.VMEM((rows, w), BF16),
                        pltpu.VMEM((rows, 1), F32), pltpu.VMEM((rows, 1), F32),
                        pltpu.VMEM((rows, w), F32),
                        pltpu.VMEM((rows, 1), F32), pltpu.VMEM((rows, 1), F32),
                        pltpu.VMEM((rows, V_DIFF), F32)],
    )
    return pl.pallas_call(
        functools.partial(_decode_kernel, pps=pps, past_len=float(n_pages * page),
                          lambda_init=lambda_init),
        grid_spec=grid_spec,
        out_shape=[jax.ShapeDtypeStruct(qa.shape, BF16)] * 2,
        compiler_params=_params(("arbitrary", "arbitrary")),
        name="decode_attention",
    )(page_table, *cache_args, *small, r_past, lam_vecs, gain)


def _layer_norm(x, g, b):
    mu = jnp.mean(x, axis=-1, keepdims=True)
    xc = x - mu
    var = jnp.mean(xc * xc, axis=-1, keepdims=True)
    return xc * lax.rsqrt(var + LN_EPS) * g + b


def _ffn_kernel(x_ref, oa_ref, ob_ref, ga_ref, gb_ref, g1_ref, sh2_ref, sc2_ref, g2_ref,
                wa_ref, wb_ref, wo_ref, ln1g_ref, ln1b_ref, wg_ref, wu_ref, wd_ref,
                ln2g_ref, ln2b_ref, y_ref, *, alpha):
    nb, r, d = x_ref.shape
    rows = nb * r

    def flat(ref):
        return ref[...].reshape(rows, ref.shape[-1])

    def per_seq(ref):
        return jnp.broadcast_to(ref[...], (nb, r, d)).reshape(rows, d)

    a = jnp.dot(flat(oa_ref), wa_ref[...], preferred_element_type=F32)
    b = jnp.dot(flat(ob_ref), wb_ref[...], preferred_element_type=F32)
    m = jax.nn.sigmoid(flat(ga_ref)) * a + jax.nn.sigmoid(flat(gb_ref)) * b
    mo = jnp.dot(m.astype(BF16), wo_ref[...], preferred_element_type=F32)
    x1 = _layer_norm(alpha * flat(x_ref) + per_seq(g1_ref) * mo, ln1g_ref[...], ln1b_ref[...])
    h2 = (x1 * (1.0 + per_seq(sc2_ref)) + per_seq(sh2_ref)).astype(BF16)
    g = jnp.dot(h2, wg_ref[...], preferred_element_type=F32)
    u = jnp.dot(h2, wu_ref[...], preferred_element_type=F32)
    act = (g * jax.nn.sigmoid(g) * u).astype(BF16)
    f = jnp.dot(act, wd_ref[...], preferred_element_type=F32)
    y = _layer_norm(alpha * x1 + per_seq(g2_ref) * f, ln2g_ref[...], ln2b_ref[...])
    y_ref[...] = y.reshape(nb, r, d)


def _merge_ffn(x, oa, ob, ga, gb, g1, sh2, sc2, g2, weights, *, seqs_per_block, rows_per_seq_block,
               alpha):
    n_seq, t, d = x.shape
    nb, r = seqs_per_block, rows_per_seq_block

    def blk(width):
        return pl.BlockSpec((nb, r, width), lambda i, j: (i, j, 0))

    mod = pl.BlockSpec((nb, 1, d), lambda i, j: (i, 0, 0))
    return pl.pallas_call(
        functools.partial(_ffn_kernel, alpha=alpha),
        grid=(n_seq // nb, t // r),
        in_specs=[blk(d), blk(W_ATT), blk(W_ATT), blk(d), blk(d), mod, mod, mod, mod]
        + [_resident(wt.shape) for wt in weights],
        out_specs=blk(d),
        out_shape=jax.ShapeDtypeStruct((n_seq, t, d), F32),
        compiler_params=_params(("arbitrary", "arbitrary")),
        name="merge_ffn",
    )(x, oa, ob, ga, gb, g1, sh2, sc2, g2, *weights)


def kernel(x_prompt, x_sample, c_prompt, c_sample, cache_k_fox, cache_v_fox, cache_logf_fox,
           cache_k_diff, cache_v_diff, page_table, w_ada, b_ada, w_in, b_forget,
           lambda_q1, lambda_k1, lambda_q2, lambda_k2, subln_gain, w_branch_a, w_branch_b,
           w_out, ln1_gain, ln1_bias, w_ffn_gate, w_ffn_up, w_ffn_down, ln2_gain, ln2_bias):
    depth = w_in.shape[0]
    batch, seq, d = x_prompt.shape
    dec_batch, dec_seq, _ = x_sample.shape
    alpha = (2 * depth) ** 0.25
    w = W_ATT
    f_start = 3 * w
    caches_t = (jnp.transpose(cache_k_fox, (0, 1, 3, 4, 2)), jnp.transpose(cache_v_fox, (0, 1, 3, 4, 2)),
                jnp.transpose(cache_k_diff, (0, 1, 3, 4, 5, 2)),
                cache_v_diff.reshape(cache_v_diff.shape[:2] + (-1, V_DIFF)))
    logf_t = jnp.transpose(cache_logf_fox, (0, 1, 3, 2))

    xp, xs = x_prompt, x_sample
    outs_p = [[] for _ in range(5)]
    outs_s = [[] for _ in range(5)]
    for l in range(depth):
        lambda_init = 0.8 - 0.6 * math.exp(-0.3 * l)
        w_in_r = jnp.concatenate(
            [w_in[l][:, :f_start], w_in[l][:, f_start + H_FOX:], w_in[l][:, f_start:f_start + H_FOX],
             jnp.zeros((d, LANES - H_FOX), F32)], axis=1).astype(BF16)
        bf_pad = jnp.concatenate([b_forget[l], jnp.zeros((LANES - H_FOX,), F32)]).reshape(1, LANES)
        lam_vecs = jnp.stack([lambda_q1[l], lambda_k1[l], lambda_q2[l], lambda_k2[l]]).astype(F32)
        gain = subln_gain[l].reshape(1, V_DIFF)
        ffn_w = (w_branch_a[l].astype(BF16), w_branch_b[l].astype(BF16), w_out[l].astype(BF16),
                 ln1_gain[l].reshape(1, d), ln1_bias[l].reshape(1, d),
                 w_ffn_gate[l].astype(BF16), w_ffn_up[l].astype(BF16), w_ffn_down[l].astype(BF16),
                 ln2_gain[l].reshape(1, d), ln2_bias[l].reshape(1, d))

        mod = _adaln(jnp.concatenate([c_prompt, c_sample], axis=0), w_ada[l], b_ada[l])
        mod_p = [mod[k, :batch].reshape(batch, 1, d) for k in range(6)]
        mod_s = [mod[k, batch:].reshape(dec_batch, 1, d) for k in range(6)]

        sh1, sc1, g1, sh2, sc2, g2 = mod_p
        (qa, ka, va, qb, kb, vb, ga, gb, logf, cum, cum_t, ka_b, va_b, kb_b, vb_b) = _project(
            xp, sc1, sh1, w_in_r, bf_pad, seqs_per_block=1, rows_per_seq_block=PROJ_ROWS,
            group=seq, carry_cum=True, emit_bf16=True)
        nblk = seq // ATT_BLOCK
        cum_rows = cum_t.reshape(batch, seq // PROJ_ROWS, SUBLANES, PROJ_ROWS)
        cum_rows = cum_rows.transpose(0, 2, 1, 3).reshape(batch, SUBLANES, nblk, ATT_BLOCK)
        oa = _fox_prompt(qa, ka_b, va_b, cum, cum_rows, blk=ATT_BLOCK)
        ob = _diff_prompt(qb, kb_b, vb_b, lam_vecs, gain, blk=ATT_BLOCK, lambda_init=lambda_init)
        xp = _merge_ffn(xp, oa, ob, ga, gb, g1, sh2, sc2, g2, ffn_w,
                        seqs_per_block=1, rows_per_seq_block=FFN_ROWS, alpha=alpha)
        for lst, val in zip(outs_p, (ka.reshape(batch, seq, H_FOX, HEAD_DIM),
                                     va.reshape(batch, seq, H_FOX, HEAD_DIM), logf,
                                     kb.reshape(batch, seq, H_DIFF, 2, HEAD_DIM),
                                     vb.reshape(batch, seq, H_DIFF, V_DIFF))):
            lst.append(val)

        sh1, sc1, g1, sh2, sc2, g2 = mod_s
        seqs_blk = PROJ_ROWS // dec_seq
        (qa, ka, va, qb, kb, vb, ga, gb, logf, cum, _unused) = _project(
            xs, sc1, sh1, w_in_r, bf_pad, seqs_per_block=seqs_blk, rows_per_seq_block=dec_seq,
            group=dec_seq, carry_cum=False, emit_bf16=False)
        cum_hq = cum.transpose(0, 2, 1)
        cq_col = cum_hq.reshape(dec_batch, H_FOX * dec_seq, 1)
        ck_new = jnp.repeat(cum_hq, dec_seq, axis=1)
        ck_new = jnp.pad(ck_new, ((0, 0), (0, 0), (0, LANES - dec_seq)))
        r_past = _suffix_sums(logf_t, l, page_table)
        oa, ob = _decode_attention(page_table, caches_t, l, qa, ka, va, qb, kb, vb, cq_col, ck_new,
                                   r_past, lam_vecs, gain, lambda_init=lambda_init)
        ffn_seqs = FFN_ROWS // dec_seq
        xs = _merge_ffn(xs, oa, ob, ga, gb, g1, sh2, sc2, g2, ffn_w,
                        seqs_per_block=ffn_seqs, rows_per_seq_block=dec_seq, alpha=alpha)
        for lst, val in zip(outs_s, (ka.reshape(dec_batch, dec_seq, H_FOX, HEAD_DIM),
                                     va.reshape(dec_batch, dec_seq, H_FOX, HEAD_DIM), logf,
                                     kb.reshape(dec_batch, dec_seq, H_DIFF, 2, HEAD_DIM),
                                     vb.reshape(dec_batch, dec_seq, H_DIFF, V_DIFF))):
            lst.append(val)

    return (xp, xs, *[jnp.stack(o, 0) for o in outs_p], *[jnp.stack(o, 0) for o in outs_s])
```

```python
import functools
import math

import jax
import jax.numpy as jnp
import numpy as np
from jax import lax
from jax.experimental import pallas as pl
from jax.experimental.pallas import tpu as pltpu

F32 = jnp.float32
BF16 = jnp.bfloat16

LANES = 128
SUBLANES = 8
HEAD_DIM = 64
H_FOX = 8
H_DIFF = 4
V_DIFF = 2 * HEAD_DIM
W_ATT = H_FOX * HEAD_DIM
LN_EPS = 1e-5
NEG_INF = -1e30
VMEM_LIMIT = 56 * 1024 * 1024

PROJ_ROWS = 512
ATT_BLOCK = 256
ATT_HEADS_PER_LOOP = 4
FFN_ROWS = 256

_HI = lax.Precision.HIGHEST
_NT = (((1,), (1,)), ((), ()))


def _log2(n):
    assert n & (n - 1) == 0, n
    return n.bit_length() - 1


def _resident(shape):
    nd = len(shape)
    return pl.BlockSpec(shape, lambda *_: (0,) * nd, pipeline_mode=pl.Buffered(1))


def _params(sem):
    return pltpu.CompilerParams(dimension_semantics=sem, vmem_limit_bytes=VMEM_LIMIT)


def _split3(x):
    hi = x.astype(BF16)
    r1 = x - hi.astype(F32)
    mid = r1.astype(BF16)
    lo = (r1 - mid.astype(F32)).astype(BF16)
    return hi, mid, lo


def _ada_kernel(c_ref, w_ref, b_ref, o_ref):
    o_ref[0] = jnp.dot(c_ref[...].astype(BF16), w_ref[...].astype(BF16),
                       preferred_element_type=F32) + b_ref[...]


def _adaln(c_all, w_ada, b_ada):
    m, d = c_all.shape
    n_chunks = w_ada.shape[1] // d
    return pl.pallas_call(
        _ada_kernel,
        grid=(n_chunks,),
        in_specs=[pl.BlockSpec((m, d), lambda j: (0, 0)),
                  pl.BlockSpec((d, d), lambda j: (0, j)),
                  pl.BlockSpec((1, d), lambda j: (0, j))],
        out_specs=pl.BlockSpec((1, m, d), lambda j: (j, 0, 0)),
        out_shape=jax.ShapeDtypeStruct((n_chunks, m, d), F32),
        compiler_params=_params(("arbitrary",)),
        name="adaln",
    )(c_all, w_ada, b_ada.reshape(1, -1))


def _proj_kernel(x_ref, sc_ref, sh_ref, w_ref, bf_ref,
                 qa_ref, ka_ref, va_ref, qb_ref, kb_ref, vb_ref, ga_ref, gb_ref,
                 logf_ref, cum_ref, cumt_ref, *rest, group, carry_cum, emit_bf16):
    if emit_bf16:
        kab_ref, vab_ref, kbb_ref, vbb_ref, carry_ref = rest
    else:
        (carry_ref,) = rest
    nb, r, d = x_ref.shape
    rows = nb * r
    h = x_ref[...] * (1.0 + sc_ref[...]) + sh_ref[...]
    hb = h.reshape(rows, d).astype(BF16)

    def col(start, width):
        return jnp.dot(hb, w_ref[:, start:start + width], preferred_element_type=F32)

    scale = HEAD_DIM ** -0.5
    w = W_ATT
    qa_ref[...] = (col(0, w) * scale).astype(BF16).reshape(nb, r, w)
    ka = col(w, w)
    ka_ref[...] = ka.reshape(nb, r, w)
    va = col(2 * w, w)
    va_ref[...] = va.reshape(nb, r, w)
    qb_ref[...] = (col(3 * w, w) * scale).astype(BF16).reshape(nb, r, w)
    kb = col(4 * w, w)
    kb_ref[...] = kb.reshape(nb, r, w)
    vb = col(5 * w, w)
    vb_ref[...] = vb.reshape(nb, r, w)
    if emit_bf16:
        kab_ref[...] = ka.astype(BF16).reshape(nb, r, w)
        vab_ref[...] = va.astype(BF16).reshape(nb, r, w)
        kbb_ref[...] = kb.astype(BF16).reshape(nb, r, w)
        vbb_ref[...] = vb.astype(BF16).reshape(nb, r, w)
    ga_ref[...] = col(6 * w, d).reshape(nb, r, d)
    gb_ref[...] = col(6 * w + d, d).reshape(nb, r, d)

    fa = col(6 * w + 2 * d, LANES) + bf_ref[...]
    logf = jax.nn.log_sigmoid(fa)
    ri = lax.broadcasted_iota(jnp.int32, (rows, rows), 0)
    ci = lax.broadcasted_iota(jnp.int32, (rows, rows), 1)
    gs = _log2(group)
    tri = jnp.where((ci <= ri) & ((ri >> gs) == (ci >> gs)), 1.0, 0.0).astype(F32)
    cum = jnp.dot(tri, logf, precision=_HI, preferred_element_type=F32)
    if carry_cum:
        @pl.when(pl.program_id(1) == 0)
        def _():
            carry_ref[...] = jnp.zeros_like(carry_ref)
        cum = cum + carry_ref[...]
        carry_ref[...] = cum[rows - 1:rows, :]
    logf_ref[...] = logf[:, :H_FOX].reshape(nb, r, H_FOX)
    cum_ref[...] = cum[:, :H_FOX].reshape(nb, r, H_FOX)
    sel = jnp.where(lax.broadcasted_iota(jnp.int32, (SUBLANES, LANES), 0)
                    == lax.broadcasted_iota(jnp.int32, (SUBLANES, LANES), 1), 1.0, 0.0).astype(F32)
    cumt_ref[0] = lax.dot_general(sel, cum, _NT, precision=_HI, preferred_element_type=F32)


def _project(x, sc, sh, w_in_r, bf_pad, *, seqs_per_block, rows_per_seq_block, group, carry_cum,
             emit_bf16):
    n_seq, t, d = x.shape
    nb, r = seqs_per_block, rows_per_seq_block
    rows = nb * r
    grid = (n_seq // nb, t // r)
    w = W_ATT

    def blk(width):
        return pl.BlockSpec((nb, r, width), lambda i, j: (i, j, 0))

    mod = pl.BlockSpec((nb, 1, d), lambda i, j: (i, 0, 0))
    n_steps = grid[0] * grid[1]
    out_shape = [jax.ShapeDtypeStruct((n_seq, t, w), BF16),
                 jax.ShapeDtypeStruct((n_seq, t, w), F32),
                 jax.ShapeDtypeStruct((n_seq, t, w), F32),
                 jax.ShapeDtypeStruct((n_seq, t, w), BF16),
                 jax.ShapeDtypeStruct((n_seq, t, w), F32),
                 jax.ShapeDtypeStruct((n_seq, t, w), F32),
                 jax.ShapeDtypeStruct((n_seq, t, d), F32),
                 jax.ShapeDtypeStruct((n_seq, t, d), F32),
                 jax.ShapeDtypeStruct((n_seq, t, H_FOX), F32),
                 jax.ShapeDtypeStruct((n_seq, t, H_FOX), F32),
                 jax.ShapeDtypeStruct((n_steps, SUBLANES, rows), F32)]
    out_specs = [blk(w), blk(w), blk(w), blk(w), blk(w), blk(w), blk(d), blk(d),
                 blk(H_FOX), blk(H_FOX),
                 pl.BlockSpec((1, SUBLANES, rows), lambda i, j: (i * grid[1] + j, 0, 0))]
    if emit_bf16:
        out_shape += [jax.ShapeDtypeStruct((n_seq, t, w), BF16)] * 4
        out_specs += [blk(w)] * 4
    return pl.pallas_call(
        functools.partial(_proj_kernel, group=group, carry_cum=carry_cum, emit_bf16=emit_bf16),
        grid=grid,
        in_specs=[blk(d), mod, mod, _resident(w_in_r.shape), _resident(bf_pad.shape)],
        out_specs=out_specs,
        out_shape=out_shape,
        scratch_shapes=[pltpu.VMEM((1, LANES), F32)],
        compiler_params=_params(("arbitrary", "arbitrary")),
        name="project",
    )(x, sc, sh, w_in_r, bf_pad)


def _online_update(s, m, l, acc, v):
    m_new = jnp.maximum(m, jnp.max(s, axis=-1, keepdims=True))
    alpha = jnp.exp(m - m_new)
    p = jnp.exp(s - m_new)
    l_new = alpha * l + jnp.sum(p, axis=-1, keepdims=True)
    acc_new = alpha * acc + jnp.dot(p.astype(BF16), v, preferred_element_type=F32)
    return m_new, l_new, acc_new


def _softmax_init(rows, width):
    return (jnp.full((rows, 1), NEG_INF, F32), jnp.zeros((rows, 1), F32), jnp.zeros((rows, width), F32))


def _fox_prompt_kernel(q_ref, k_ref, v_ref, cq_ref, ck_ref, o_ref, *, blk, heads_per_loop):
    i = pl.program_id(1)
    lane = lax.broadcasted_iota(jnp.int32, (blk, LANES), 1)
    lo = lane < HEAD_DIM
    ri = lax.broadcasted_iota(jnp.int32, (blk, blk), 0)
    ci = lax.broadcasted_iota(jnp.int32, (blk, blk), 1)
    causal = ci <= ri

    def pair_cols(hd):
        return slice((hd // 2) * LANES, (hd // 2 + 1) * LANES)

    def head_query(hd):
        q2 = q_ref[0, :, pair_cols(hd)].astype(F32)
        keep = lo if hd % 2 == 0 else jnp.logical_not(lo)
        return jnp.where(keep, q2, 0.0).astype(BF16)

    for first in range(0, H_FOX, heads_per_loop):
        heads = list(range(first, first + heads_per_loop))
        qs = [head_query(hd) for hd in heads]
        cqs = [cq_ref[0, :, hd:hd + 1] for hd in heads]

        def step(j, carry, masked, heads=heads, qs=qs, cqs=cqs):
            start = pl.multiple_of(j * blk, blk)
            out = []
            for hd, qh, cq, (m, l, acc) in zip(heads, qs, cqs, carry):
                ks = k_ref[0, pl.ds(start, blk), pair_cols(hd)]
                vs = v_ref[0, pl.ds(start, blk), pair_cols(hd)]
                s = lax.dot_general(qh, ks, _NT, preferred_element_type=F32)
                s = s + (cq - ck_ref[0, hd, pl.ds(j, 1), :])
                if masked:
                    s = jnp.where(causal, s, NEG_INF)
                out.append(_online_update(s, m, l, acc, vs))
            return tuple(out)

        init = tuple(_softmax_init(blk, LANES) for _ in heads)
        carry = lax.fori_loop(0, i, functools.partial(step, masked=False), init)
        final = step(i, carry, True)
        for k in range(0, heads_per_loop, 2):
            (_, l0, a0), (_, l1, a1) = final[k], final[k + 1]
            o_ref[0, :, pair_cols(first + k)] = jnp.where(lo, a0 / l0, a1 / l1).astype(BF16)


def _fox_prompt(q, k, v, cum, cum_t, *, blk):
    b, t, w = q.shape
    nblk = t // blk
    return pl.pallas_call(
        functools.partial(_fox_prompt_kernel, blk=blk, heads_per_loop=ATT_HEADS_PER_LOOP),
        grid=(b, nblk),
        in_specs=[pl.BlockSpec((1, blk, w), lambda bi, i: (bi, i, 0)),
                  pl.BlockSpec((1, t, w), lambda bi, i: (bi, 0, 0)),
                  pl.BlockSpec((1, t, w), lambda bi, i: (bi, 0, 0)),
                  pl.BlockSpec((1, blk, H_FOX), lambda bi, i: (bi, i, 0)),
                  pl.BlockSpec((1, H_FOX, nblk, blk), lambda bi, i: (bi, 0, 0, 0))],
        out_specs=pl.BlockSpec((1, blk, w), lambda bi, i: (bi, i, 0)),
        out_shape=jax.ShapeDtypeStruct((b, t, w), BF16),
        compiler_params=_params(("arbitrary", "arbitrary")),
        name="fox_prompt",
    )(q, k, v, cum, cum_t)


def _diff_lambda(lam_ref, lambda_init):
    lv = lam_ref[...]
    s1 = jnp.sum(lv[0:1] * lv[1:2], axis=-1, keepdims=True)
    s2 = jnp.sum(lv[2:3] * lv[3:4], axis=-1, keepdims=True)
    return jnp.exp(s1) - jnp.exp(s2) + lambda_init


def _subln(o, gain, lambda_init):
    ms = jnp.mean(o * o, axis=-1, keepdims=True)
    return o * lax.rsqrt(ms + LN_EPS) * gain * (1.0 - lambda_init)


def _alibi_slope(hd):
    return 2.0 ** (-8.0 * (hd + 1) / H_DIFF)


def _diff_prompt_kernel(q_ref, k_ref, v_ref, lam_ref, gain_ref, o_ref, *, blk, lambda_init,
                        heads_per_loop):
    i = pl.program_id(1)
    lane = lax.broadcasted_iota(jnp.int32, (blk, LANES), 1)
    lo = lane < HEAD_DIM
    ri = lax.broadcasted_iota(jnp.int32, (blk, blk), 0)
    ci = lax.broadcasted_iota(jnp.int32, (blk, blk), 1)
    causal = ci <= ri
    rel = (ci - ri).astype(F32)
    lam = _diff_lambda(lam_ref, lambda_init)

    for first in range(0, H_DIFF, heads_per_loop):
        heads = list(range(first, first + heads_per_loop))
        qmaps = []
        for hd in heads:
            q2 = q_ref[0, :, hd * LANES:(hd + 1) * LANES].astype(F32)
            qmaps.append((jnp.where(lo, q2, 0.0).astype(BF16), jnp.where(lo, 0.0, q2).astype(BF16)))

        def step(j, carry, masked, heads=heads, qmaps=qmaps):
            start = pl.multiple_of(j * blk, blk)
            out = []
            for hd, (q_1, q_2), (c1, c2) in zip(heads, qmaps, carry):
                cols = slice(hd * LANES, (hd + 1) * LANES)
                slope = _alibi_slope(hd)
                ks = k_ref[0, pl.ds(start, blk), cols]
                vs = v_ref[0, pl.ds(start, blk), cols]
                bias = slope * rel + (slope * blk) * (j - i).astype(F32)
                s1 = lax.dot_general(q_1, ks, _NT, preferred_element_type=F32) + bias
                s2 = lax.dot_general(q_2, ks, _NT, preferred_element_type=F32) + bias
                if masked:
                    s1 = jnp.where(causal, s1, NEG_INF)
                    s2 = jnp.where(causal, s2, NEG_INF)
                out.append((_online_update(s1, *c1, vs), _online_update(s2, *c2, vs)))
            return tuple(out)

        init = tuple((_softmax_init(blk, LANES), _softmax_init(blk, LANES)) for _ in heads)
        carry = lax.fori_loop(0, i, functools.partial(step, masked=False), init)
        final = step(i, carry, True)
        for hd, ((_, l1, a1), (_, l2, a2)) in zip(heads, final):
            o = a1 / l1 - lam * (a2 / l2)
            o_ref[0, :, hd * LANES:(hd + 1) * LANES] = _subln(o, gain_ref[...], lambda_init).astype(BF16)


def _diff_prompt(q, k, v, lam_vecs, gain, *, blk, lambda_init):
    b, t, w = q.shape
    return pl.pallas_call(
        functools.partial(_diff_prompt_kernel, blk=blk, lambda_init=lambda_init,
                          heads_per_loop=ATT_HEADS_PER_LOOP // 2),
        grid=(b, t // blk),
        in_specs=[pl.BlockSpec((1, blk, w), lambda bi, i: (bi, i, 0)),
                  pl.BlockSpec((1, t, w), lambda bi, i: (bi, 0, 0)),
                  pl.BlockSpec((1, t, w), lambda bi, i: (bi, 0, 0)),
                  pl.BlockSpec(lam_vecs.shape, lambda bi, i: (0, 0)),
                  pl.BlockSpec(gain.shape, lambda bi, i: (0, 0))],
        out_specs=pl.BlockSpec((1, blk, w), lambda bi, i: (bi, i, 0)),
        out_shape=jax.ShapeDtypeStruct((b, t, w), BF16),
        compiler_params=_params(("arbitrary", "arbitrary")),
        name="diff_prompt",
    )(q, k, v, lam_vecs, gain)


def _suffix_weights(page):
    t_in = np.arange(page)[:, None]
    suffix = t_in > np.arange(page)[None, :]
    return np.concatenate([suffix, np.ones_like(suffix)], axis=1).astype(np.float32)


def _suffix_sums(page_refs, w_ref, out_scr):
    n_pages = len(page_refs)
    hf, page = page_refs[0].shape
    lrow = jnp.concatenate([pr[...] for pr in page_refs], axis=0)
    wm = w_ref[...]
    y = sum(jnp.dot(piece, wm, preferred_element_type=F32) for piece in _split3(lrow))
    off = jnp.zeros((hf, page), F32)
    for p in reversed(range(n_pages)):
        out_scr[p] = y[p * hf:(p + 1) * hf, :page] + off
        off = off + y[p * hf:(p + 1) * hf, page:]


def _block_diag_queries(q):
    nq, w = q.shape
    groups = w // HEAD_DIM
    tiled = jnp.broadcast_to(q.astype(F32)[None], (groups, nq, w)).reshape(groups * nq, w)
    ri = lax.broadcasted_iota(jnp.int32, (groups * nq, w), 0)
    ci = lax.broadcasted_iota(jnp.int32, (groups * nq, w), 1)
    return jnp.where((ri >> _log2(nq)) == (ci >> _log2(HEAD_DIM)), tiled, 0.0).astype(BF16)


def _softmax_parts(scores):
    m = functools.reduce(jnp.maximum, [jnp.max(s, axis=-1, keepdims=True) for s in scores])
    ps = [jnp.exp(s - m) for s in scores]
    l = functools.reduce(jnp.add, [jnp.sum(p, axis=-1, keepdims=True) for p in ps])
    return ps, l


def _decode_kernel(pt_ref, *refs, n_pages, lambda_init):
    kf_refs = refs[0:n_pages]
    vf_refs = refs[n_pages:2 * n_pages]
    kd_refs = refs[2 * n_pages:3 * n_pages]
    vd_refs = refs[3 * n_pages:4 * n_pages]
    lf_refs = refs[4 * n_pages:5 * n_pages]
    (w_ref, qa_ref, ka_ref, va_ref, qb_ref, kb_ref, vb_ref, cq_ref, ckn_ref, lam_ref, gain_ref,
     oa_ref, ob_ref, r_scr) = refs[5 * n_pages:]
    nq = qa_ref.shape[1]
    page = kf_refs[0].shape[-1]
    rows = H_FOX * nq
    past_len = n_pages * page

    _suffix_sums(lf_refs, w_ref, r_scr)
    qf = _block_diag_queries(qa_ref[0])
    qd = _block_diag_queries(qb_ref[0])

    def pages_t(page_refs):
        return jnp.concatenate([pr[...].reshape(W_ATT, page) for pr in page_refs], axis=1).astype(BF16)

    pad = jnp.zeros((LANES - nq, W_ATT), F32)

    def padded(ref):
        return jnp.concatenate([ref[0], pad], axis=0).astype(BF16)

    row = lax.broadcasted_iota(jnp.int32, (rows, 1), 0)
    q_of_row = (row & (nq - 1)).astype(F32)
    slope_row = jnp.zeros((rows, 1), F32)
    for hd in range(H_DIFF):
        slope_row = jnp.where((row >> _log2(2 * nq)) == hd, _alibi_slope(hd), slope_row)
    col = lax.broadcasted_iota(jnp.int32, (rows, LANES), 1)
    visible = col <= (row & (nq - 1))

    s_past = jnp.dot(qf, pages_t(kf_refs), preferred_element_type=F32)
    r_tile = jnp.concatenate(
        [jnp.concatenate([jnp.broadcast_to(r_scr[p, hd:hd + 1, :], (nq, page)) for p in range(n_pages)],
                         axis=1) for hd in range(H_FOX)], axis=0)
    s_past = s_past + (cq_ref[0] + r_tile)
    s_new = lax.dot_general(qf, padded(ka_ref), _NT, preferred_element_type=F32)
    s_new = jnp.where(visible, s_new + (cq_ref[0] - ckn_ref[0]), NEG_INF)
    (p_past, p_new), l = _softmax_parts([s_past, s_new])
    acc = (lax.dot_general(p_past.astype(BF16), pages_t(vf_refs), _NT, preferred_element_type=F32)
           + jnp.dot(p_new.astype(BF16), padded(va_ref), preferred_element_type=F32))
    of = acc / l
    lane = lax.broadcasted_iota(jnp.int32, (nq, W_ATT), 1)
    oa = jnp.zeros((nq, W_ATT), F32)
    for hd in range(H_FOX):
        oa = jnp.where((lane >> _log2(HEAD_DIM)) == hd, of[hd * nq:(hd + 1) * nq, :], oa)
    oa_ref[0] = oa.astype(BF16)

    s_past = jnp.dot(qd, pages_t(kd_refs), preferred_element_type=F32)
    key_pos = lax.broadcasted_iota(jnp.int32, (rows, past_len), 1).astype(F32)
    s_past = s_past + slope_row * (key_pos - (past_len + q_of_row))
    s_new = lax.dot_general(qd, padded(kb_ref), _NT, preferred_element_type=F32)
    s_new = jnp.where(visible, s_new + slope_row * (col.astype(F32) - q_of_row), NEG_INF)
    (p_past, p_new), l = _softmax_parts([s_past, s_new])
    vb_new = padded(vb_ref)
    per_head = 2 * nq
    heads = []
    for hd in range(H_DIFF):
        rs = slice(hd * per_head, (hd + 1) * per_head)
        v_past = jnp.concatenate([pr[pl.ds(hd, page, stride=H_DIFF), :] for pr in vd_refs],
                                 axis=0).astype(BF16)
        heads.append(jnp.dot(p_past[rs].astype(BF16), v_past, preferred_element_type=F32)
                     + jnp.dot(p_new[rs].astype(BF16), vb_new[:, hd * V_DIFF:(hd + 1) * V_DIFF],
                               preferred_element_type=F32))
    od = jnp.concatenate(heads, axis=0) / l
    lam = _diff_lambda(lam_ref, lambda_init)
    outs = []
    for hd in range(H_DIFF):
        r0 = hd * per_head
        o = od[r0:r0 + nq] - lam * od[r0 + nq:r0 + per_head]
        outs.append(_subln(o, gain_ref[...], lambda_init))
    ob_ref[0] = jnp.concatenate(outs, axis=1).astype(BF16)


def _decode_attention(page_table, caches_t, logf_t, layer, qa, ka, va, qb, kb, vb, cq_col, ck_new,
                      lam_vecs, gain, *, lambda_init):
    n_seq, nq, w = qa.shape
    n_pages = page_table.shape[1]
    hf, page = logf_t.shape[2:]
    wmat = jnp.asarray(_suffix_weights(page), dtype=BF16)

    def page_spec(cache, p):
        tail = cache.shape[2:]
        return pl.BlockSpec((None, None) + tail,
                            functools.partial(lambda b, pt, p: (layer, pt[b, p]) + (0,) * len(tail), p=p))

    def per_seq(shape):
        nd = len(shape)
        return pl.BlockSpec((1,) + tuple(shape[1:]), lambda b, pt: (b,) + (0,) * (nd - 1))

    paged = list(caches_t) + [logf_t]
    cache_specs = [page_spec(c, p) for c in paged for p in range(n_pages)]
    cache_args = [c for c in paged for _ in range(n_pages)]
    small = [qa, ka, va, qb, kb, vb, cq_col, ck_new]
    grid_spec = pltpu.PrefetchScalarGridSpec(
        num_scalar_prefetch=1,
        grid=(n_seq,),
        in_specs=cache_specs + [pl.BlockSpec(wmat.shape, lambda b, pt: (0, 0))]
        + [per_seq(a.shape) for a in small]
        + [pl.BlockSpec(lam_vecs.shape, lambda b, pt: (0, 0)),
           pl.BlockSpec(gain.shape, lambda b, pt: (0, 0))],
        out_specs=[per_seq(qa.shape), per_seq(qa.shape)],
        scratch_shapes=[pltpu.VMEM((n_pages, hf, page), F32)],
    )
    return pl.pallas_call(
        functools.partial(_decode_kernel, n_pages=n_pages, lambda_init=lambda_init),
        grid_spec=grid_spec,
        out_shape=[jax.ShapeDtypeStruct(qa.shape, BF16)] * 2,
        compiler_params=_params(("arbitrary",)),
        name="decode_attention",
    )(page_table, *cache_args, wmat, *small, lam_vecs, gain)


def _layer_norm(x, g, b):
    mu = jnp.mean(x, axis=-1, keepdims=True)
    xc = x - mu
    var = jnp.mean(xc * xc, axis=-1, keepdims=True)
    return xc * lax.rsqrt(var + LN_EPS) * g + b


def _ffn_kernel(x_ref, oa_ref, ob_ref, ga_ref, gb_ref, g1_ref, sh2_ref, sc2_ref, g2_ref,
                wa_ref, wb_ref, wo_ref, ln1g_ref, ln1b_ref, wg_ref, wu_ref, wd_ref,
                ln2g_ref, ln2b_ref, y_ref, *, alpha):
    nb, r, d = x_ref.shape
    rows = nb * r

    def flat(ref):
        return ref[...].reshape(rows, ref.shape[-1])

    def per_seq(ref):
        return jnp.broadcast_to(ref[...], (nb, r, d)).reshape(rows, d)

    a = jnp.dot(flat(oa_ref), wa_ref[...], preferred_element_type=F32)
    b = jnp.dot(flat(ob_ref), wb_ref[...], preferred_element_type=F32)
    m = jax.nn.sigmoid(flat(ga_ref)) * a + jax.nn.sigmoid(flat(gb_ref)) * b
    mo = jnp.dot(m.astype(BF16), wo_ref[...], preferred_element_type=F32)
    x1 = _layer_norm(alpha * flat(x_ref) + per_seq(g1_ref) * mo, ln1g_ref[...], ln1b_ref[...])
    h2 = (x1 * (1.0 + per_seq(sc2_ref)) + per_seq(sh2_ref)).astype(BF16)
    g = jnp.dot(h2, wg_ref[...], preferred_element_type=F32)
    u = jnp.dot(h2, wu_ref[...], preferred_element_type=F32)
    act = (g * jax.nn.sigmoid(g) * u).astype(BF16)
    f = jnp.dot(act, wd_ref[...], preferred_element_type=F32)
    y = _layer_norm(alpha * x1 + per_seq(g2_ref) * f, ln2g_ref[...], ln2b_ref[...])
    y_ref[...] = y.reshape(nb, r, d)


def _merge_ffn(x, oa, ob, ga, gb, g1, sh2, sc2, g2, weights, *, seqs_per_block, rows_per_seq_block,
               alpha):
    n_seq, t, d = x.shape
    nb, r = seqs_per_block, rows_per_seq_block

    def blk(width):
        return pl.BlockSpec((nb, r, width), lambda i, j: (i, j, 0))

    mod = pl.BlockSpec((nb, 1, d), lambda i, j: (i, 0, 0))
    return pl.pallas_call(
        functools.partial(_ffn_kernel, alpha=alpha),
        grid=(n_seq // nb, t // r),
        in_specs=[blk(d), blk(W_ATT), blk(W_ATT), blk(d), blk(d), mod, mod, mod, mod]
        + [_resident(wt.shape) for wt in weights],
        out_specs=blk(d),
        out_shape=jax.ShapeDtypeStruct((n_seq, t, d), F32),
        compiler_params=_params(("arbitrary", "arbitrary")),
        name="merge_ffn",
    )(x, oa, ob, ga, gb, g1, sh2, sc2, g2, *weights)


def kernel(x_prompt, x_sample, c_prompt, c_sample, cache_k_fox, cache_v_fox, cache_logf_fox,
           cache_k_diff, cache_v_diff, page_table, w_ada, b_ada, w_in, b_forget,
           lambda_q1, lambda_k1, lambda_q2, lambda_k2, subln_gain, w_branch_a, w_branch_b,
           w_out, ln1_gain, ln1_bias, w_ffn_gate, w_ffn_up, w_ffn_down, ln2_gain, ln2_bias):
    depth = w_in.shape[0]
    batch, seq, d = x_prompt.shape
    dec_batch, dec_seq, _ = x_sample.shape
    alpha = (2 * depth) ** 0.25
    w = W_ATT
    f_start = 3 * w
    caches_t = (jnp.transpose(cache_k_fox, (0, 1, 3, 4, 2)), jnp.transpose(cache_v_fox, (0, 1, 3, 4, 2)),
                jnp.transpose(cache_k_diff, (0, 1, 3, 4, 5, 2)),
                cache_v_diff.reshape(cache_v_diff.shape[:2] + (-1, V_DIFF)))
    logf_t = jnp.transpose(cache_logf_fox, (0, 1, 3, 2))

    xp, xs = x_prompt, x_sample
    outs_p = [[] for _ in range(5)]
    outs_s = [[] for _ in range(5)]
    for l in range(depth):
        lambda_init = 0.8 - 0.6 * math.exp(-0.3 * l)
        w_in_r = jnp.concatenate(
            [w_in[l][:, :f_start], w_in[l][:, f_start + H_FOX:], w_in[l][:, f_start:f_start + H_FOX],
             jnp.zeros((d, LANES - H_FOX), F32)], axis=1).astype(BF16)
        bf_pad = jnp.concatenate([b_forget[l], jnp.zeros((LANES - H_FOX,), F32)]).reshape(1, LANES)
        lam_vecs = jnp.stack([lambda_q1[l], lambda_k1[l], lambda_q2[l], lambda_k2[l]]).astype(F32)
        gain = subln_gain[l].reshape(1, V_DIFF)
        ffn_w = (w_branch_a[l].astype(BF16), w_branch_b[l].astype(BF16), w_out[l].astype(BF16),
                 ln1_gain[l].reshape(1, d), ln1_bias[l].reshape(1, d),
                 w_ffn_gate[l].astype(BF16), w_ffn_up[l].astype(BF16), w_ffn_down[l].astype(BF16),
                 ln2_gain[l].reshape(1, d), ln2_bias[l].reshape(1, d))

        mod = _adaln(jnp.concatenate([c_prompt, c_sample], axis=0), w_ada[l], b_ada[l])
        mod_p = [mod[k, :batch].reshape(batch, 1, d) for k in range(6)]
        mod_s = [mod[k, batch:].reshape(dec_batch, 1, d) for k in range(6)]

        sh1, sc1, g1, sh2, sc2, g2 = mod_p
        (qa, ka, va, qb, kb, vb, ga, gb, logf, cum, cum_t, ka_b, va_b, kb_b, vb_b) = _project(
            xp, sc1, sh1, w_in_r, bf_pad, seqs_per_block=1, rows_per_seq_block=PROJ_ROWS,
            group=seq, carry_cum=True, emit_bf16=True)
        nblk = seq // ATT_BLOCK
        cum_rows = cum_t.reshape(batch, seq // PROJ_ROWS, SUBLANES, PROJ_ROWS)
        cum_rows = cum_rows.transpose(0, 2, 1, 3).reshape(batch, SUBLANES, nblk, ATT_BLOCK)
        oa = _fox_prompt(qa, ka_b, va_b, cum, cum_rows, blk=ATT_BLOCK)
        ob = _diff_prompt(qb, kb_b, vb_b, lam_vecs, gain, blk=ATT_BLOCK, lambda_init=lambda_init)
        xp = _merge_ffn(xp, oa, ob, ga, gb, g1, sh2, sc2, g2, ffn_w,
                        seqs_per_block=1, rows_per_seq_block=FFN_ROWS, alpha=alpha)
        for lst, val in zip(outs_p, (ka.reshape(batch, seq, H_FOX, HEAD_DIM),
                                     va.reshape(batch, seq, H_FOX, HEAD_DIM), logf,
                                     kb.reshape(batch, seq, H_DIFF, 2, HEAD_DIM),
                                     vb.reshape(batch, seq, H_DIFF, V_DIFF))):
            lst.append(val)

        sh1, sc1, g1, sh2, sc2, g2 = mod_s
        seqs_blk = PROJ_ROWS // dec_seq
        (qa, ka, va, qb, kb, vb, ga, gb, logf, cum, _unused) = _project(
            xs, sc1, sh1, w_in_r, bf_pad, seqs_per_block=seqs_blk, rows_per_seq_block=dec_seq,
            group=dec_seq, carry_cum=False, emit_bf16=False)
        cum_hq = cum.transpose(0, 2, 1)
        cq_col = cum_hq.reshape(dec_batch, H_FOX * dec_seq, 1)
        ck_new = jnp.repeat(cum_hq, dec_seq, axis=1)
        ck_new = jnp.pad(ck_new, ((0, 0), (0, 0), (0, LANES - dec_seq)))
        oa, ob = _decode_attention(page_table, caches_t, logf_t, l, qa, ka, va, qb, kb, vb, cq_col, ck_new,
                                   lam_vecs, gain, lambda_init=lambda_init)
        ffn_seqs = FFN_ROWS // dec_seq
        xs = _merge_ffn(xs, oa, ob, ga, gb, g1, sh2, sc2, g2, ffn_w,
                        seqs_per_block=ffn_seqs, rows_per_seq_block=dec_seq, alpha=alpha)
        for lst, val in zip(outs_s, (ka.reshape(dec_batch, dec_seq, H_FOX, HEAD_DIM),
                                     va.reshape(dec_batch, dec_seq, H_FOX, HEAD_DIM), logf,
                                     kb.reshape(dec_batch, dec_seq, H_DIFF, 2, HEAD_DIM),
                                     vb.reshape(dec_batch, dec_seq, H_DIFF, V_DIFF))):
            lst.append(val)

    return (xp, xs, *[jnp.stack(o, 0) for o in outs_p], *[jnp.stack(o, 0) for o in outs_s])
```

```python
import functools
import math

import jax
import jax.numpy as jnp
import numpy as np
from jax import lax
from jax.experimental import pallas as pl
from jax.experimental.pallas import tpu as pltpu

F32 = jnp.float32
BF16 = jnp.bfloat16

LANES = 128
SUBLANES = 8
HEAD_DIM = 64
H_FOX = 8
H_DIFF = 4
V_DIFF = 2 * HEAD_DIM
W_ATT = H_FOX * HEAD_DIM
LN_EPS = 1e-5
NEG_INF = -1e30
LOG2E = math.log2(math.e)
VMEM_LIMIT = 56 * 1024 * 1024

PROJ_ROWS = 512
ATT_BLOCK = 512
FOX_HEADS_PER_LOOP = 2
DIFF_HEADS_PER_LOOP = 1
BIAS_LANES = 8
FFN_ROWS = 256

_HI = lax.Precision.HIGHEST
_NT = (((1,), (1,)), ((), ()))


def _log2(n):
    assert n & (n - 1) == 0, n
    return n.bit_length() - 1


def _resident(shape):
    nd = len(shape)
    return pl.BlockSpec(shape, lambda *_: (0,) * nd, pipeline_mode=pl.Buffered(1))


def _params(sem):
    return pltpu.CompilerParams(dimension_semantics=sem, vmem_limit_bytes=VMEM_LIMIT)


def _split3(x):
    hi = x.astype(BF16)
    r1 = x - hi.astype(F32)
    mid = r1.astype(BF16)
    lo = (r1 - mid.astype(F32)).astype(BF16)
    return hi, mid, lo


def _ada_kernel(c_ref, w_ref, b_ref, o_ref):
    o_ref[0] = jnp.dot(c_ref[...].astype(BF16), w_ref[...].astype(BF16),
                       preferred_element_type=F32) + b_ref[...]


def _adaln(c_all, w_ada, b_ada):
    m, d = c_all.shape
    n_chunks = w_ada.shape[1] // d
    return pl.pallas_call(
        _ada_kernel,
        grid=(n_chunks,),
        in_specs=[pl.BlockSpec((m, d), lambda j: (0, 0)),
                  pl.BlockSpec((d, d), lambda j: (0, j)),
                  pl.BlockSpec((1, d), lambda j: (0, j))],
        out_specs=pl.BlockSpec((1, m, d), lambda j: (j, 0, 0)),
        out_shape=jax.ShapeDtypeStruct((n_chunks, m, d), F32),
        compiler_params=_params(("arbitrary",)),
        name="adaln",
    )(c_all, w_ada, b_ada.reshape(1, -1))


def _fox_bias_map():
    e = np.zeros((3 * LANES, 2 * LANES), np.float32)
    ones = np.zeros((1, 2 * LANES), np.float32)
    for hd in range(H_FOX):
        for k in range(3):
            e[LANES * k + hd, BIAS_LANES * hd + k] = -1.0
            e[LANES * k + hd, LANES + BIAS_LANES * hd + 3 + k] = 1.0
            ones[0, BIAS_LANES * hd + 3 + k] = 1.0
            ones[0, LANES + BIAS_LANES * hd + k] = 1.0
    return e, ones


def _proj_kernel(x_ref, sc_ref, sh_ref, w_ref, bf_ref, *rest, group, prompt, fox_q_scale):
    if prompt:
        (emap_ref, eones_ref, qa_ref, ka_ref, va_ref, qb_ref, kb_ref, vb_ref, ga_ref, gb_ref, logf_ref,
         kab_ref, vab_ref, kbb_ref, vbb_ref, bq_ref, bk_ref, carry_ref) = rest
    else:
        (qa_ref, ka_ref, va_ref, qb_ref, kb_ref, vb_ref, ga_ref, gb_ref, logf_ref, cum_ref) = rest
    nb, r, d = x_ref.shape
    rows = nb * r
    h = x_ref[...] * (1.0 + sc_ref[...]) + sh_ref[...]
    hb = h.reshape(rows, d).astype(BF16)

    def col(start, width):
        return jnp.dot(hb, w_ref[:, start:start + width], preferred_element_type=F32)

    scale = HEAD_DIM ** -0.5
    w = W_ATT
    qa_ref[...] = (col(0, w) * fox_q_scale).astype(BF16).reshape(nb, r, w)
    ka = col(w, w)
    ka_ref[...] = ka.reshape(nb, r, w)
    va = col(2 * w, w)
    va_ref[...] = va.reshape(nb, r, w)
    qb_ref[...] = (col(3 * w, w) * scale).astype(BF16).reshape(nb, r, w)
    kb = col(4 * w, w)
    kb_ref[...] = kb.reshape(nb, r, w)
    vb = col(5 * w, w)
    vb_ref[...] = vb.reshape(nb, r, w)
    if prompt:
        kab_ref[...] = ka.astype(BF16).reshape(nb, r, w)
        vab_ref[...] = va.astype(BF16).reshape(nb, r, w)
        kbb_ref[...] = kb.astype(BF16).reshape(nb, r, w)
        vbb_ref[...] = vb.astype(BF16).reshape(nb, r, w)
    ga_ref[...] = col(6 * w, d).reshape(nb, r, d)
    gb_ref[...] = col(6 * w + d, d).reshape(nb, r, d)

    fa = col(6 * w + 2 * d, LANES) + bf_ref[...]
    logf = jax.nn.log_sigmoid(fa)
    logf_ref[...] = logf[:, :H_FOX].reshape(nb, r, H_FOX)
    ri = lax.broadcasted_iota(jnp.int32, (rows, rows), 0)
    ci = lax.broadcasted_iota(jnp.int32, (rows, rows), 1)
    gs = _log2(group)
    tri = jnp.where((ci <= ri) & ((ri >> gs) == (ci >> gs)), 1.0, 0.0).astype(F32)
    cum = jnp.dot(tri, logf, precision=_HI, preferred_element_type=F32)
    if not prompt:
        cum_ref[...] = cum[:, :H_FOX].reshape(nb, r, H_FOX)
        return
    @pl.when(pl.program_id(1) == 0)
    def _():
        carry_ref[...] = jnp.zeros_like(carry_ref)
    cum = cum + carry_ref[...]
    carry_ref[...] = cum[rows - 1:rows, :]
    pieces = jnp.concatenate(_split3(cum * LOG2E), axis=1)
    placed = jnp.dot(pieces, emap_ref[...], preferred_element_type=F32) + eones_ref[...]
    bk_ref[...] = placed[:, :LANES].astype(BF16).reshape(nb, r, LANES)
    bq_ref[...] = placed[:, LANES:].astype(BF16).reshape(nb, r, LANES)


def _project(x, sc, sh, w_in_r, bf_pad, *, seqs_per_block, rows_per_seq_block, group, prompt):
    n_seq, t, d = x.shape
    nb, r = seqs_per_block, rows_per_seq_block
    grid = (n_seq // nb, t // r)
    w = W_ATT

    def blk(width):
        return pl.BlockSpec((nb, r, width), lambda i, j: (i, j, 0))

    def out(width, dtype):
        return jax.ShapeDtypeStruct((n_seq, t, width), dtype)

    mod = pl.BlockSpec((nb, 1, d), lambda i, j: (i, 0, 0))
    out_shape = [out(w, BF16), out(w, F32), out(w, F32), out(w, BF16), out(w, F32), out(w, F32),
                 out(d, F32), out(d, F32), out(H_FOX, F32)]
    out_specs = [blk(w)] * 6 + [blk(d), blk(d), blk(H_FOX)]
    inputs = [x, sc, sh, w_in_r, bf_pad]
    in_specs = [blk(d), mod, mod, _resident(w_in_r.shape), _resident(bf_pad.shape)]
    scratch = []
    if prompt:
        emap, eones = _fox_bias_map()
        inputs += [jnp.asarray(emap, dtype=BF16), jnp.asarray(eones)]
        in_specs += [_resident(emap.shape), _resident(eones.shape)]
        out_shape += [out(w, BF16)] * 4 + [out(LANES, BF16)] * 2
        out_specs += [blk(w)] * 4 + [blk(LANES)] * 2
        scratch = [pltpu.VMEM((1, LANES), F32)]
    else:
        out_shape += [out(H_FOX, F32)]
        out_specs += [blk(H_FOX)]
    return pl.pallas_call(
        functools.partial(_proj_kernel, group=group, prompt=prompt,
                          fox_q_scale=HEAD_DIM ** -0.5 * (LOG2E if prompt else 1.0)),
        grid=grid,
        in_specs=in_specs,
        out_specs=out_specs,
        out_shape=out_shape,
        scratch_shapes=scratch,
        compiler_params=_params(("arbitrary", "arbitrary")),
        name="project",
    )(*inputs)


def _online_update(s, m, l, acc, v):
    m_new = jnp.maximum(m, jnp.max(s, axis=-1, keepdims=True))
    alpha = jnp.exp(m - m_new)
    p = jnp.exp(s - m_new)
    l_new = alpha * l + jnp.sum(p, axis=-1, keepdims=True)
    acc_new = alpha * acc + jnp.dot(p.astype(BF16), v, preferred_element_type=F32)
    return m_new, l_new, acc_new


def _softmax_init(rows, width):
    return (jnp.full((rows, 1), NEG_INF, F32), jnp.zeros((rows, 1), F32), jnp.zeros((rows, width), F32))


def _online_update2(s, m, l, acc, v):
    m_new = jnp.maximum(m, jnp.max(s, axis=-1, keepdims=True))
    alpha = jnp.exp2(m - m_new)
    p = jnp.exp2(s - m_new)
    l_new = alpha * l + jnp.sum(p, axis=-1, keepdims=True)
    acc_new = alpha * acc + jnp.dot(p.astype(BF16), v, preferred_element_type=F32)
    return m_new, l_new, acc_new


def _fox_prompt_kernel(q_ref, k_ref, v_ref, bq_ref, ak_ref, o_ref, *, blk, heads_per_loop):
    i = pl.program_id(1)
    lane = lax.broadcasted_iota(jnp.int32, (blk, LANES), 1)
    lo = lane < HEAD_DIM
    ri = lax.broadcasted_iota(jnp.int32, (blk, blk), 0)
    ci = lax.broadcasted_iota(jnp.int32, (blk, blk), 1)
    causal = ci <= ri
    bq = bq_ref[0].astype(F32)

    def pair_cols(hd):
        return slice((hd // 2) * LANES, (hd // 2 + 1) * LANES)

    def head_query(hd):
        q2 = q_ref[0, :, pair_cols(hd)].astype(F32)
        keep = lo if hd % 2 == 0 else jnp.logical_not(lo)
        own = (lane >> _log2(BIAS_LANES)) == hd
        return jnp.concatenate([jnp.where(keep, q2, 0.0), jnp.where(own, bq, 0.0)],
                               axis=1).astype(BF16)

    for first in range(0, H_FOX, heads_per_loop):
        heads = list(range(first, first + heads_per_loop))
        qs = [head_query(hd) for hd in heads]

        def step(j, carry, masked, heads=heads, qs=qs):
            start = pl.multiple_of(j * blk, blk)
            a_blk = ak_ref[0, pl.ds(start, blk), :]
            out = []
            for hd, qh, (m, l, acc) in zip(heads, qs, carry):
                ks = jnp.concatenate([k_ref[0, pl.ds(start, blk), pair_cols(hd)], a_blk], axis=1)
                vs = v_ref[0, pl.ds(start, blk), pair_cols(hd)]
                s = lax.dot_general(qh, ks, _NT, preferred_element_type=F32)
                if masked:
                    s = jnp.where(causal, s, NEG_INF)
                out.append(_online_update2(s, m, l, acc, vs))
            return tuple(out)

        init = tuple(_softmax_init(blk, LANES) for _ in heads)
        carry = lax.fori_loop(0, i, functools.partial(step, masked=False), init)
        final = step(i, carry, True)
        for k in range(0, heads_per_loop, 2):
            (_, l0, a0), (_, l1, a1) = final[k], final[k + 1]
            o_ref[0, :, pair_cols(first + k)] = jnp.where(lo, a0 / l0, a1 / l1).astype(BF16)


def _fox_prompt(q, k, v, bias_q, bias_k, *, blk):
    b, t, w = q.shape
    return pl.pallas_call(
        functools.partial(_fox_prompt_kernel, blk=blk, heads_per_loop=FOX_HEADS_PER_LOOP),
        grid=(b, t // blk),
        in_specs=[pl.BlockSpec((1, blk, w), lambda bi, i: (bi, i, 0)),
                  pl.BlockSpec((1, t, w), lambda bi, i: (bi, 0, 0)),
                  pl.BlockSpec((1, t, w), lambda bi, i: (bi, 0, 0)),
                  pl.BlockSpec((1, blk, LANES), lambda bi, i: (bi, i, 0)),
                  pl.BlockSpec((1, t, LANES), lambda bi, i: (bi, 0, 0))],
        out_specs=pl.BlockSpec((1, blk, w), lambda bi, i: (bi, i, 0)),
        out_shape=jax.ShapeDtypeStruct((b, t, w), BF16),
        compiler_params=_params(("arbitrary", "arbitrary")),
        name="fox_prompt",
    )(q, k, v, bias_q, bias_k)


def _diff_lambda(lam_ref, lambda_init):
    lv = lam_ref[...]
    s1 = jnp.sum(lv[0:1] * lv[1:2], axis=-1, keepdims=True)
    s2 = jnp.sum(lv[2:3] * lv[3:4], axis=-1, keepdims=True)
    return jnp.exp(s1) - jnp.exp(s2) + lambda_init


def _subln(o, gain, lambda_init):
    ms = jnp.mean(o * o, axis=-1, keepdims=True)
    return o * lax.rsqrt(ms + LN_EPS) * gain * (1.0 - lambda_init)


def _alibi_slope(hd):
    return 2.0 ** (-8.0 * (hd + 1) / H_DIFF)


def _alibi_key_lanes(t):
    s = np.arange(t)
    lanes = np.zeros((t, LANES), np.float32)
    lanes[:, 0], lanes[:, 1], lanes[:, 2], lanes[:, 3] = s >> 8, s & 255, 1.0, 1.0
    return lanes


def _diff_prompt_kernel(q_ref, k_ref, v_ref, pos_ref, lam_ref, gain_ref, o_ref, *, blk, lambda_init,
                        heads_per_loop):
    i = pl.program_id(1)
    lane = lax.broadcasted_iota(jnp.int32, (blk, LANES), 1)
    lo = lane < HEAD_DIM
    ri = lax.broadcasted_iota(jnp.int32, (blk, blk), 0)
    ci = lax.broadcasted_iota(jnp.int32, (blk, blk), 1)
    causal = ci <= ri
    lam = _diff_lambda(lam_ref, lambda_init)
    t = i * blk + lax.broadcasted_iota(jnp.int32, (blk, LANES), 0)
    t_hi = (t >> 8).astype(F32)
    t_lo = (t & 255).astype(F32)

    def head_queries(hd):
        slope = _alibi_slope(hd)
        pos = jnp.where(lane == 0, 256.0 * slope,
                        jnp.where(lane == 1, slope,
                                  jnp.where(lane == 2, (-256.0 * slope) * t_hi,
                                            jnp.where(lane == 3, -slope * t_lo, 0.0))))
        q2 = q_ref[0, :, hd * LANES:(hd + 1) * LANES].astype(F32)
        return (jnp.concatenate([jnp.where(lo, q2, 0.0), pos], axis=1).astype(BF16),
                jnp.concatenate([jnp.where(lo, 0.0, q2), pos], axis=1).astype(BF16))

    for first in range(0, H_DIFF, heads_per_loop):
        heads = list(range(first, first + heads_per_loop))
        qmaps = [head_queries(hd) for hd in heads]

        def step(j, carry, masked, heads=heads, qmaps=qmaps):
            start = pl.multiple_of(j * blk, blk)
            pos_k = pos_ref[pl.ds(start, blk), :]
            out = []
            for hd, (q_1, q_2), (c1, c2) in zip(heads, qmaps, carry):
                cols = slice(hd * LANES, (hd + 1) * LANES)
                ks = jnp.concatenate([k_ref[0, pl.ds(start, blk), cols], pos_k], axis=1)
                vs = v_ref[0, pl.ds(start, blk), cols]
                s1 = lax.dot_general(q_1, ks, _NT, preferred_element_type=F32)
                s2 = lax.dot_general(q_2, ks, _NT, preferred_element_type=F32)
                if masked:
                    s1 = jnp.where(causal, s1, NEG_INF)
                    s2 = jnp.where(causal, s2, NEG_INF)
                out.append((_online_update(s1, *c1, vs), _online_update(s2, *c2, vs)))
            return tuple(out)

        init = tuple((_softmax_init(blk, LANES), _softmax_init(blk, LANES)) for _ in heads)
        carry = lax.fori_loop(0, i, functools.partial(step, masked=False), init)
        final = step(i, carry, True)
        for hd, ((_, l1, a1), (_, l2, a2)) in zip(heads, final):
            o = a1 / l1 - lam * (a2 / l2)
            o_ref[0, :, hd * LANES:(hd + 1) * LANES] = _subln(o, gain_ref[...], lambda_init).astype(BF16)


def _diff_prompt(q, k, v, lam_vecs, gain, *, blk, lambda_init):
    b, t, w = q.shape
    pos_k = jnp.asarray(_alibi_key_lanes(t), dtype=BF16)
    return pl.pallas_call(
        functools.partial(_diff_prompt_kernel, blk=blk, lambda_init=lambda_init,
                          heads_per_loop=DIFF_HEADS_PER_LOOP),
        grid=(b, t // blk),
        in_specs=[pl.BlockSpec((1, blk, w), lambda bi, i: (bi, i, 0)),
                  pl.BlockSpec((1, t, w), lambda bi, i: (bi, 0, 0)),
                  pl.BlockSpec((1, t, w), lambda bi, i: (bi, 0, 0)),
                  pl.BlockSpec(pos_k.shape, lambda bi, i: (0, 0)),
                  pl.BlockSpec(lam_vecs.shape, lambda bi, i: (0, 0)),
                  pl.BlockSpec(gain.shape, lambda bi, i: (0, 0))],
        out_specs=pl.BlockSpec((1, blk, w), lambda bi, i: (bi, i, 0)),
        out_shape=jax.ShapeDtypeStruct((b, t, w), BF16),
        compiler_params=_params(("arbitrary", "arbitrary")),
        name="diff_prompt",
    )(q, k, v, pos_k, lam_vecs, gain)


def _suffix_weights(page):
    t_in = np.arange(page)[:, None]
    suffix = t_in > np.arange(page)[None, :]
    return np.concatenate([suffix, np.ones_like(suffix)], axis=1).astype(np.float32)


def _suffix_sums(page_refs, w_ref, out_scr):
    n_pages = len(page_refs)
    hf, page = page_refs[0].shape
    lrow = jnp.concatenate([pr[...] for pr in page_refs], axis=0)
    wm = w_ref[...]
    y = sum(jnp.dot(piece, wm, preferred_element_type=F32) for piece in _split3(lrow))
    off = jnp.zeros((hf, page), F32)
    for p in reversed(range(n_pages)):
        out_scr[p] = y[p * hf:(p + 1) * hf, :page] + off
        off = off + y[p * hf:(p + 1) * hf, page:]


def _block_diag_queries(q):
    nq, w = q.shape
    groups = w // HEAD_DIM
    tiled = jnp.broadcast_to(q.astype(F32)[None], (groups, nq, w)).reshape(groups * nq, w)
    ri = lax.broadcasted_iota(jnp.int32, (groups * nq, w), 0)
    ci = lax.broadcasted_iota(jnp.int32, (groups * nq, w), 1)
    return jnp.where((ri >> _log2(nq)) == (ci >> _log2(HEAD_DIM)), tiled, 0.0).astype(BF16)


def _softmax_parts(scores):
    m = functools.reduce(jnp.maximum, [jnp.max(s, axis=-1, keepdims=True) for s in scores])
    ps = [jnp.exp(s - m) for s in scores]
    l = functools.reduce(jnp.add, [jnp.sum(p, axis=-1, keepdims=True) for p in ps])
    return ps, l


def _decode_kernel(pt_ref, *refs, n_pages, lambda_init):
    kf_refs = refs[0:n_pages]
    vf_refs = refs[n_pages:2 * n_pages]
    kd_refs = refs[2 * n_pages:3 * n_pages]
    vd_refs = refs[3 * n_pages:4 * n_pages]
    lf_refs = refs[4 * n_pages:5 * n_pages]
    (w_ref, qa_ref, ka_ref, va_ref, qb_ref, kb_ref, vb_ref, cq_ref, ckn_ref, lam_ref, gain_ref,
     oa_ref, ob_ref, r_scr) = refs[5 * n_pages:]
    nq = qa_ref.shape[1]
    page = kf_refs[0].shape[-1]
    rows = H_FOX * nq
    past_len = n_pages * page

    _suffix_sums(lf_refs, w_ref, r_scr)
    qf = _block_diag_queries(qa_ref[0])
    qd = _block_diag_queries(qb_ref[0])

    def pages_t(page_refs):
        return jnp.concatenate([pr[...].reshape(W_ATT, page) for pr in page_refs], axis=1).astype(BF16)

    pad = jnp.zeros((LANES - nq, W_ATT), F32)

    def padded(ref):
        return jnp.concatenate([ref[0], pad], axis=0).astype(BF16)

    row = lax.broadcasted_iota(jnp.int32, (rows, 1), 0)
    q_of_row = (row & (nq - 1)).astype(F32)
    slope_row = jnp.zeros((rows, 1), F32)
    for hd in range(H_DIFF):
        slope_row = jnp.where((row >> _log2(2 * nq)) == hd, _alibi_slope(hd), slope_row)
    col = lax.broadcasted_iota(jnp.int32, (rows, LANES), 1)
    visible = col <= (row & (nq - 1))

    s_past = jnp.dot(qf, pages_t(kf_refs), preferred_element_type=F32)
    r_tile = jnp.concatenate(
        [jnp.concatenate([jnp.broadcast_to(r_scr[p, hd:hd + 1, :], (nq, page)) for p in range(n_pages)],
                         axis=1) for hd in range(H_FOX)], axis=0)
    s_past = s_past + (cq_ref[0] + r_tile)
    s_new = lax.dot_general(qf, padded(ka_ref), _NT, preferred_element_type=F32)
    s_new = jnp.where(visible, s_new + (cq_ref[0] - ckn_ref[0]), NEG_INF)
    (p_past, p_new), l = _softmax_parts([s_past, s_new])
    acc = (lax.dot_general(p_past.astype(BF16), pages_t(vf_refs), _NT, preferred_element_type=F32)
           + jnp.dot(p_new.astype(BF16), padded(va_ref), preferred_element_type=F32))
    of = acc / l
    lane = lax.broadcasted_iota(jnp.int32, (nq, W_ATT), 1)
    oa = jnp.zeros((nq, W_ATT), F32)
    for hd in range(H_FOX):
        oa = jnp.where((lane >> _log2(HEAD_DIM)) == hd, of[hd * nq:(hd + 1) * nq, :], oa)
    oa_ref[0] = oa.astype(BF16)

    s_past = jnp.dot(qd, pages_t(kd_refs), preferred_element_type=F32)
    key_pos = lax.broadcasted_iota(jnp.int32, (rows, past_len), 1).astype(F32)
    s_past = s_past + slope_row * (key_pos - (past_len + q_of_row))
    s_new = lax.dot_general(qd, padded(kb_ref), _NT, preferred_element_type=F32)
    s_new = jnp.where(visible, s_new + slope_row * (col.astype(F32) - q_of_row), NEG_INF)
    (p_past, p_new), l = _softmax_parts([s_past, s_new])
    vb_new = padded(vb_ref)
    per_head = 2 * nq
    heads = []
    for hd in range(H_DIFF):
        rs = slice(hd * per_head, (hd + 1) * per_head)
        v_past = jnp.concatenate([pr[pl.ds(hd, page, stride=H_DIFF), :] for pr in vd_refs],
                                 axis=0).astype(BF16)
        heads.append(jnp.dot(p_past[rs].astype(BF16), v_past, preferred_element_type=F32)
                     + jnp.dot(p_new[rs].astype(BF16), vb_new[:, hd * V_DIFF:(hd + 1) * V_DIFF],
                               preferred_element_type=F32))
    od = jnp.concatenate(heads, axis=0) / l
    lam = _diff_lambda(lam_ref, lambda_init)
    outs = []
    for hd in range(H_DIFF):
        r0 = hd * per_head
        o = od[r0:r0 + nq] - lam * od[r0 + nq:r0 + per_head]
        outs.append(_subln(o, gain_ref[...], lambda_init))
    ob_ref[0] = jnp.concatenate(outs, axis=1).astype(BF16)


def _decode_attention(page_table, caches_t, logf_t, layer, qa, ka, va, qb, kb, vb, cq_col, ck_new,
                      lam_vecs, gain, *, lambda_init):
    n_seq, nq, w = qa.shape
    n_pages = page_table.shape[1]
    hf, page = logf_t.shape[2:]
    wmat = jnp.asarray(_suffix_weights(page), dtype=BF16)

    def page_spec(cache, p):
        tail = cache.shape[2:]
        return pl.BlockSpec((None, None) + tail,
                            functools.partial(lambda b, pt, p: (layer, pt[b, p]) + (0,) * len(tail), p=p))

    def per_seq(shape):
        nd = len(shape)
        return pl.BlockSpec((1,) + tuple(shape[1:]), lambda b, pt: (b,) + (0,) * (nd - 1))

    paged = list(caches_t) + [logf_t]
    cache_specs = [page_spec(c, p) for c in paged for p in range(n_pages)]
    cache_args = [c for c in paged for _ in range(n_pages)]
    small = [qa, ka, va, qb, kb, vb, cq_col, ck_new]
    grid_spec = pltpu.PrefetchScalarGridSpec(
        num_scalar_prefetch=1,
        grid=(n_seq,),
        in_specs=cache_specs + [pl.BlockSpec(wmat.shape, lambda b, pt: (0, 0))]
        + [per_seq(a.shape) for a in small]
        + [pl.BlockSpec(lam_vecs.shape, lambda b, pt: (0, 0)),
           pl.BlockSpec(gain.shape, lambda b, pt: (0, 0))],
        out_specs=[per_seq(qa.shape), per_seq(qa.shape)],
        scratch_shapes=[pltpu.VMEM((n_pages, hf, page), F32)],
    )
    return pl.pallas_call(
        functools.partial(_decode_kernel, n_pages=n_pages, lambda_init=lambda_init),
        grid_spec=grid_spec,
        out_shape=[jax.ShapeDtypeStruct(qa.shape, BF16)] * 2,
        compiler_params=_params(("arbitrary",)),
        name="decode_attention",
    )(page_table, *cache_args, wmat, *small, lam_vecs, gain)


def _layer_norm(x, g, b):
    mu = jnp.mean(x, axis=-1, keepdims=True)
    xc = x - mu
    var = jnp.mean(xc * xc, axis=-1, keepdims=True)
    return xc * lax.rsqrt(var + LN_EPS) * g + b


def _ffn_kernel(x_ref, oa_ref, ob_ref, ga_ref, gb_ref, g1_ref, sh2_ref, sc2_ref, g2_ref,
                wa_ref, wb_ref, wo_ref, ln1g_ref, ln1b_ref, wg_ref, wu_ref, wd_ref,
                ln2g_ref, ln2b_ref, y_ref, *, alpha):
    nb, r, d = x_ref.shape
    rows = nb * r

    def flat(ref):
        return ref[...].reshape(rows, ref.shape[-1])

    def per_seq(ref):
        return jnp.broadcast_to(ref[...], (nb, r, d)).reshape(rows, d)

    a = jnp.dot(flat(oa_ref), wa_ref[...], preferred_element_type=F32)
    b = jnp.dot(flat(ob_ref), wb_ref[...], preferred_element_type=F32)
    m = jax.nn.sigmoid(flat(ga_ref)) * a + jax.nn.sigmoid(flat(gb_ref)) * b
    mo = jnp.dot(m.astype(BF16), wo_ref[...], preferred_element_type=F32)
    x1 = _layer_norm(alpha * flat(x_ref) + per_seq(g1_ref) * mo, ln1g_ref[...], ln1b_ref[...])
    h2 = (x1 * (1.0 + per_seq(sc2_ref)) + per_seq(sh2_ref)).astype(BF16)
    g = jnp.dot(h2, wg_ref[...], preferred_element_type=F32)
    u = jnp.dot(h2, wu_ref[...], preferred_element_type=F32)
    act = (g * jax.nn.sigmoid(g) * u).astype(BF16)
    f = jnp.dot(act, wd_ref[...], preferred_element_type=F32)
    y = _layer_norm(alpha * x1 + per_seq(g2_ref) * f, ln2g_ref[...], ln2b_ref[...])
    y_ref[...] = y.reshape(nb, r, d)


def _merge_ffn(x, oa, ob, ga, gb, g1, sh2, sc2, g2, weights, *, seqs_per_block, rows_per_seq_block,
               alpha):
    n_seq, t, d = x.shape
    nb, r = seqs_per_block, rows_per_seq_block

    def blk(width):
        return pl.BlockSpec((nb, r, width), lambda i, j: (i, j, 0))

    mod = pl.BlockSpec((nb, 1, d), lambda i, j: (i, 0, 0))
    return pl.pallas_call(
        functools.partial(_ffn_kernel, alpha=alpha),
        grid=(n_seq // nb, t // r),
        in_specs=[blk(d), blk(W_ATT), blk(W_ATT), blk(d), blk(d), mod, mod, mod, mod]
        + [_resident(wt.shape) for wt in weights],
        out_specs=blk(d),
        out_shape=jax.ShapeDtypeStruct((n_seq, t, d), F32),
        compiler_params=_params(("arbitrary", "arbitrary")),
        name="merge_ffn",
    )(x, oa, ob, ga, gb, g1, sh2, sc2, g2, *weights)


def kernel(x_prompt, x_sample, c_prompt, c_sample, cache_k_fox, cache_v_fox, cache_logf_fox,
           cache_k_diff, cache_v_diff, page_table, w_ada, b_ada, w_in, b_forget,
           lambda_q1, lambda_k1, lambda_q2, lambda_k2, subln_gain, w_branch_a, w_branch_b,
           w_out, ln1_gain, ln1_bias, w_ffn_gate, w_ffn_up, w_ffn_down, ln2_gain, ln2_bias):
    depth = w_in.shape[0]
    batch, seq, d = x_prompt.shape
    dec_batch, dec_seq, _ = x_sample.shape
    alpha = (2 * depth) ** 0.25
    w = W_ATT
    f_start = 3 * w
    caches_t = (jnp.transpose(cache_k_fox, (0, 1, 3, 4, 2)), jnp.transpose(cache_v_fox, (0, 1, 3, 4, 2)),
                jnp.transpose(cache_k_diff, (0, 1, 3, 4, 5, 2)),
                cache_v_diff.reshape(cache_v_diff.shape[:2] + (-1, V_DIFF)))
    logf_t = jnp.transpose(cache_logf_fox, (0, 1, 3, 2))

    xp, xs = x_prompt, x_sample
    outs_p = [[] for _ in range(5)]
    outs_s = [[] for _ in range(5)]
    for l in range(depth):
        lambda_init = 0.8 - 0.6 * math.exp(-0.3 * l)
        w_in_r = jnp.concatenate(
            [w_in[l][:, :f_start], w_in[l][:, f_start + H_FOX:], w_in[l][:, f_start:f_start + H_FOX],
             jnp.zeros((d, LANES - H_FOX), F32)], axis=1).astype(BF16)
        bf_pad = jnp.concatenate([b_forget[l], jnp.zeros((LANES - H_FOX,), F32)]).reshape(1, LANES)
        lam_vecs = jnp.stack([lambda_q1[l], lambda_k1[l], lambda_q2[l], lambda_k2[l]]).astype(F32)
        gain = subln_gain[l].reshape(1, V_DIFF)
        ffn_w = (w_branch_a[l].astype(BF16), w_branch_b[l].astype(BF16), w_out[l].astype(BF16),
                 ln1_gain[l].reshape(1, d), ln1_bias[l].reshape(1, d),
                 w_ffn_gate[l].astype(BF16), w_ffn_up[l].astype(BF16), w_ffn_down[l].astype(BF16),
                 ln2_gain[l].reshape(1, d), ln2_bias[l].reshape(1, d))

        mod = _adaln(jnp.concatenate([c_prompt, c_sample], axis=0), w_ada[l], b_ada[l])
        mod_p = [mod[k, :batch].reshape(batch, 1, d) for k in range(6)]
        mod_s = [mod[k, batch:].reshape(dec_batch, 1, d) for k in range(6)]

        sh1, sc1, g1, sh2, sc2, g2 = mod_p
        (qa, ka, va, qb, kb, vb, ga, gb, logf, ka_b, va_b, kb_b, vb_b, bias_q, bias_k) = _project(
            xp, sc1, sh1, w_in_r, bf_pad, seqs_per_block=1, rows_per_seq_block=PROJ_ROWS,
            group=seq, prompt=True)
        oa = _fox_prompt(qa, ka_b, va_b, bias_q, bias_k, blk=ATT_BLOCK)
        ob = _diff_prompt(qb, kb_b, vb_b, lam_vecs, gain, blk=ATT_BLOCK, lambda_init=lambda_init)
        xp = _merge_ffn(xp, oa, ob, ga, gb, g1, sh2, sc2, g2, ffn_w,
                        seqs_per_block=1, rows_per_seq_block=FFN_ROWS, alpha=alpha)
        for lst, val in zip(outs_p, (ka.reshape(batch, seq, H_FOX, HEAD_DIM),
                                     va.reshape(batch, seq, H_FOX, HEAD_DIM), logf,
                                     kb.reshape(batch, seq, H_DIFF, 2, HEAD_DIM),
                                     vb.reshape(batch, seq, H_DIFF, V_DIFF))):
            lst.append(val)

        sh1, sc1, g1, sh2, sc2, g2 = mod_s
        seqs_blk = PROJ_ROWS // dec_seq
        (qa, ka, va, qb, kb, vb, ga, gb, logf, cum) = _project(
            xs, sc1, sh1, w_in_r, bf_pad, seqs_per_block=seqs_blk, rows_per_seq_block=dec_seq,
            group=dec_seq, prompt=False)
        cum_hq = cum.transpose(0, 2, 1)
        cq_col = cum_hq.reshape(dec_batch, H_FOX * dec_seq, 1)
        ck_new = jnp.repeat(cum_hq, dec_seq, axis=1)
        ck_new = jnp.pad(ck_new, ((0, 0), (0, 0), (0, LANES - dec_seq)))
        oa, ob = _decode_attention(page_table, caches_t, logf_t, l, qa, ka, va, qb, kb, vb, cq_col, ck_new,
                                   lam_vecs, gain, lambda_init=lambda_init)
        ffn_seqs = FFN_ROWS // dec_seq
        xs = _merge_ffn(xs, oa, ob, ga, gb, g1, sh2, sc2, g2, ffn_w,
                        seqs_per_block=ffn_seqs, rows_per_seq_block=dec_seq, alpha=alpha)
        for lst, val in zip(outs_s, (ka.reshape(dec_batch, dec_seq, H_FOX, HEAD_DIM),
                                     va.reshape(dec_batch, dec_seq, H_FOX, HEAD_DIM), logf,
                                     kb.reshape(dec_batch, dec_seq, H_DIFF, 2, HEAD_DIM),
                                     vb.reshape(dec_batch, dec_seq, H_DIFF, V_DIFF))):
            lst.append(val)

    return (xp, xs, *[jnp.stack(o, 0) for o in outs_p], *[jnp.stack(o, 0) for o in outs_s])
```

```python
import functools
import math

import jax
import jax.numpy as jnp
import numpy as np
from jax import lax
from jax.experimental import pallas as pl
from jax.experimental.pallas import tpu as pltpu

F32 = jnp.float32
BF16 = jnp.bfloat16

LANES = 128
HEAD_DIM = 64
H_FOX = 8
H_DIFF = 4
V_DIFF = 2 * HEAD_DIM
W_ATT = H_FOX * HEAD_DIM
LN_EPS = 1e-5
NEG_INF = -1e30
LOG2E = math.log2(math.e)
VMEM_LIMIT = 56 * 1024 * 1024

PROJ_ROWS = 512
ATT_BLOCK = 512
FOX_HEADS_PER_LOOP = 2
DIFF_HEADS_PER_LOOP = 1
BIAS_LANES = 8
FFN_ROWS = 256

_HI = lax.Precision.HIGHEST
_NT = (((1,), (1,)), ((), ()))


def _log2(n):
    assert n & (n - 1) == 0, n
    return n.bit_length() - 1


def _resident(shape):
    nd = len(shape)
    return pl.BlockSpec(shape, lambda *_: (0,) * nd, pipeline_mode=pl.Buffered(1))


def _params(sem):
    return pltpu.CompilerParams(dimension_semantics=sem, vmem_limit_bytes=VMEM_LIMIT)


def _split3(x):
    hi = x.astype(BF16)
    r1 = x - hi.astype(F32)
    mid = r1.astype(BF16)
    lo = (r1 - mid.astype(F32)).astype(BF16)
    return hi, mid, lo


def _ada_kernel(c_ref, w_ref, b_ref, o_ref):
    o_ref[0] = jnp.dot(c_ref[...].astype(BF16), w_ref[...].astype(BF16),
                       preferred_element_type=F32) + b_ref[...]


def _adaln(c_all, w_ada, b_ada):
    m, d = c_all.shape
    n_chunks = w_ada.shape[1] // d
    return pl.pallas_call(
        _ada_kernel,
        grid=(n_chunks,),
        in_specs=[pl.BlockSpec((m, d), lambda j: (0, 0)),
                  pl.BlockSpec((d, d), lambda j: (0, j)),
                  pl.BlockSpec((1, d), lambda j: (0, j))],
        out_specs=pl.BlockSpec((1, m, d), lambda j: (j, 0, 0)),
        out_shape=jax.ShapeDtypeStruct((n_chunks, m, d), F32),
        compiler_params=_params(("arbitrary",)),
        name="adaln",
    )(c_all, w_ada, b_ada.reshape(1, -1))


def _fox_bias_map():
    e = np.zeros((3 * LANES, 2 * LANES), np.float32)
    ones = np.zeros((1, 2 * LANES), np.float32)
    for hd in range(H_FOX):
        for k in range(3):
            e[LANES * k + hd, BIAS_LANES * hd + k] = -1.0
            e[LANES * k + hd, LANES + BIAS_LANES * hd + 3 + k] = 1.0
            ones[0, BIAS_LANES * hd + 3 + k] = 1.0
            ones[0, LANES + BIAS_LANES * hd + k] = 1.0
    return e, ones


def _proj_kernel(x_ref, sc_ref, sh_ref, w_ref, bf_ref, *rest, group, prompt, fox_q_scale):
    if prompt:
        (emap_ref, eones_ref, qa_ref, ka_ref, va_ref, qb_ref, kb_ref, vb_ref, ga_ref, gb_ref, logf_ref,
         kab_ref, vab_ref, kbb_ref, vbb_ref, bq_ref, bk_ref, carry_ref) = rest
    else:
        (qa_ref, ka_ref, va_ref, qb_ref, kb_ref, vb_ref, ga_ref, gb_ref, logf_ref, cum_ref) = rest
    nb, r, d = x_ref.shape
    rows = nb * r
    h = x_ref[...] * (1.0 + sc_ref[...]) + sh_ref[...]
    hb = h.reshape(rows, d).astype(BF16)

    def col(start, width):
        return jnp.dot(hb, w_ref[:, start:start + width], preferred_element_type=F32)

    scale = HEAD_DIM ** -0.5
    w = W_ATT
    qa_ref[...] = (col(0, w) * fox_q_scale).astype(BF16).reshape(nb, r, w)
    ka = col(w, w)
    va = col(2 * w, w)
    qb_ref[...] = (col(3 * w, w) * scale).astype(BF16).reshape(nb, r, w)
    kb = col(4 * w, w)
    vb = col(5 * w, w)
    if prompt:
        ka_ref[0] = ka.T
        va_ref[0] = va.T
        kb_ref[0] = kb.T
        for hd in range(H_DIFF):
            vb_ref[0, pl.ds(hd, rows, stride=H_DIFF), :] = vb[:, hd * V_DIFF:(hd + 1) * V_DIFF]
        kab_ref[...] = ka.astype(BF16).reshape(nb, r, w)
        vab_ref[...] = va.astype(BF16).reshape(nb, r, w)
        kbb_ref[...] = kb.astype(BF16).reshape(nb, r, w)
        vbb_ref[...] = vb.astype(BF16).reshape(nb, r, w)
    else:
        ka_ref[...] = ka.reshape(nb, r, w)
        va_ref[...] = va.reshape(nb, r, w)
        kb_ref[...] = kb.reshape(nb, r, w)
        vb_ref[...] = vb.reshape(nb, r, w)
    ga_ref[...] = jax.nn.sigmoid(col(6 * w, d)).astype(BF16).reshape(nb, r, d)
    gb_ref[...] = jax.nn.sigmoid(col(6 * w + d, d)).astype(BF16).reshape(nb, r, d)

    fa = col(6 * w + 2 * d, LANES) + bf_ref[...]
    logf = jax.nn.log_sigmoid(fa)
    logf_ref[...] = logf[:, :H_FOX].reshape(nb, r, H_FOX)
    ri = lax.broadcasted_iota(jnp.int32, (rows, rows), 0)
    ci = lax.broadcasted_iota(jnp.int32, (rows, rows), 1)
    gs = _log2(group)
    tri = jnp.where((ci <= ri) & ((ri >> gs) == (ci >> gs)), 1.0, 0.0).astype(F32)
    cum = jnp.dot(tri, logf, precision=_HI, preferred_element_type=F32)
    if not prompt:
        cum_ref[...] = cum[:, :H_FOX].reshape(nb, r, H_FOX)
        return
    @pl.when(pl.program_id(1) == 0)
    def _():
        carry_ref[...] = jnp.zeros_like(carry_ref)
    cum = cum + carry_ref[...]
    carry_ref[...] = cum[rows - 1:rows, :]
    pieces = jnp.concatenate(_split3(cum * LOG2E), axis=1)
    placed = jnp.dot(pieces, emap_ref[...], preferred_element_type=F32) + eones_ref[...]
    bk_ref[...] = placed[:, :LANES].astype(BF16).reshape(nb, r, LANES)
    bq_ref[...] = placed[:, LANES:].astype(BF16).reshape(nb, r, LANES)


def _project(x, sc, sh, w_in_r, bf_pad, *, seqs_per_block, rows_per_seq_block, group, prompt):
    n_seq, t, d = x.shape
    nb, r = seqs_per_block, rows_per_seq_block
    grid = (n_seq // nb, t // r)
    w = W_ATT

    def blk(width):
        return pl.BlockSpec((nb, r, width), lambda i, j: (i, j, 0))

    def out(width, dtype):
        return jax.ShapeDtypeStruct((n_seq, t, width), dtype)

    mod = pl.BlockSpec((nb, 1, d), lambda i, j: (i, 0, 0))
    if prompt:
        kv_t = jax.ShapeDtypeStruct((n_seq, w, t), F32)
        kv_t_spec = pl.BlockSpec((nb, w, r), lambda i, j: (i, 0, j))
        vb_rows = jax.ShapeDtypeStruct((n_seq, t * H_DIFF, V_DIFF), F32)
        vb_rows_spec = pl.BlockSpec((nb, r * H_DIFF, V_DIFF), lambda i, j: (i, j, 0))
        kv_shapes = [kv_t, kv_t, kv_t, vb_rows]
        kv_specs = [kv_t_spec, kv_t_spec, kv_t_spec, vb_rows_spec]
    else:
        kv_shapes = [out(w, F32)] * 4
        kv_specs = [blk(w)] * 4
    out_shape = [out(w, BF16), kv_shapes[0], kv_shapes[1], out(w, BF16), kv_shapes[2], kv_shapes[3],
                 out(d, BF16), out(d, BF16), out(H_FOX, F32)]
    out_specs = [blk(w), kv_specs[0], kv_specs[1], blk(w), kv_specs[2], kv_specs[3],
                 blk(d), blk(d), blk(H_FOX)]
    inputs = [x, sc, sh, w_in_r, bf_pad]
    in_specs = [blk(d), mod, mod, _resident(w_in_r.shape), _resident(bf_pad.shape)]
    scratch = []
    if prompt:
        emap, eones = _fox_bias_map()
        inputs += [jnp.asarray(emap, dtype=BF16), jnp.asarray(eones)]
        in_specs += [_resident(emap.shape), _resident(eones.shape)]
        out_shape += [out(w, BF16)] * 4 + [out(LANES, BF16)] * 2
        out_specs += [blk(w)] * 4 + [blk(LANES)] * 2
        scratch = [pltpu.VMEM((1, LANES), F32)]
    else:
        out_shape += [out(H_FOX, F32)]
        out_specs += [blk(H_FOX)]
    return pl.pallas_call(
        functools.partial(_proj_kernel, group=group, prompt=prompt,
                          fox_q_scale=HEAD_DIM ** -0.5 * (LOG2E if prompt else 1.0)),
        grid=grid,
        in_specs=in_specs,
        out_specs=out_specs,
        out_shape=out_shape,
        scratch_shapes=scratch,
        compiler_params=_params(("arbitrary", "arbitrary")),
        name="project",
    )(*inputs)


def _online_update(s, m, l, acc, v):
    m_new = jnp.maximum(m, jnp.max(s, axis=-1, keepdims=True))
    alpha = jnp.exp(m - m_new)
    p = jnp.exp(s - m_new)
    l_new = alpha * l + jnp.sum(p, axis=-1, keepdims=True)
    acc_new = alpha * acc + jnp.dot(p.astype(BF16), v, preferred_element_type=F32)
    return m_new, l_new, acc_new


def _softmax_init(rows, width):
    return (jnp.full((rows, 1), NEG_INF, F32), jnp.zeros((rows, 1), F32), jnp.zeros((rows, width), F32))


def _online_update2(s, m, l, acc, v):
    m_new = jnp.maximum(m, jnp.max(s, axis=-1, keepdims=True))
    alpha = jnp.exp2(m - m_new)
    p = jnp.exp2(s - m_new)
    l_new = alpha * l + jnp.sum(p, axis=-1, keepdims=True)
    acc_new = alpha * acc + jnp.dot(p.astype(BF16), v, preferred_element_type=F32)
    return m_new, l_new, acc_new


def _fox_prompt_kernel(q_ref, k_ref, v_ref, bq_ref, ak_ref, o_ref, *, blk, heads_per_loop):
    i = pl.program_id(1)
    lane = lax.broadcasted_iota(jnp.int32, (blk, LANES), 1)
    lo = lane < HEAD_DIM
    ri = lax.broadcasted_iota(jnp.int32, (blk, blk), 0)
    ci = lax.broadcasted_iota(jnp.int32, (blk, blk), 1)
    causal = ci <= ri
    bq = bq_ref[0].astype(F32)

    def pair_cols(hd):
        return slice((hd // 2) * LANES, (hd // 2 + 1) * LANES)

    def head_query(hd):
        q2 = q_ref[0, :, pair_cols(hd)].astype(F32)
        keep = lo if hd % 2 == 0 else jnp.logical_not(lo)
        own = (lane >> _log2(BIAS_LANES)) == hd
        return jnp.concatenate([jnp.where(keep, q2, 0.0), jnp.where(own, bq, 0.0)],
                               axis=1).astype(BF16)

    for first in range(0, H_FOX, heads_per_loop):
        heads = list(range(first, first + heads_per_loop))
        qs = [head_query(hd) for hd in heads]

        def step(j, carry, masked, heads=heads, qs=qs):
            start = pl.multiple_of(j * blk, blk)
            a_blk = ak_ref[0, pl.ds(start, blk), :]
            out = []
            for hd, qh, (m, l, acc) in zip(heads, qs, carry):
                ks = jnp.concatenate([k_ref[0, pl.ds(start, blk), pair_cols(hd)], a_blk], axis=1)
                vs = v_ref[0, pl.ds(start, blk), pair_cols(hd)]
                s = lax.dot_general(qh, ks, _NT, preferred_element_type=F32)
                if masked:
                    s = jnp.where(causal, s, NEG_INF)
                out.append(_online_update2(s, m, l, acc, vs))
            return tuple(out)

        init = tuple(_softmax_init(blk, LANES) for _ in heads)
        carry = lax.fori_loop(0, i, functools.partial(step, masked=False), init)
        final = step(i, carry, True)
        for k in range(0, heads_per_loop, 2):
            (_, l0, a0), (_, l1, a1) = final[k], final[k + 1]
            o_ref[0, :, pair_cols(first + k)] = jnp.where(lo, a0 / l0, a1 / l1).astype(BF16)


def _fox_prompt(q, k, v, bias_q, bias_k, *, blk):
    b, t, w = q.shape
    return pl.pallas_call(
        functools.partial(_fox_prompt_kernel, blk=blk, heads_per_loop=FOX_HEADS_PER_LOOP),
        grid=(b, t // blk),
        in_specs=[pl.BlockSpec((1, blk, w), lambda bi, i: (bi, i, 0)),
                  pl.BlockSpec((1, t, w), lambda bi, i: (bi, 0, 0)),
                  pl.BlockSpec((1, t, w), lambda bi, i: (bi, 0, 0)),
                  pl.BlockSpec((1, blk, LANES), lambda bi, i: (bi, i, 0)),
                  pl.BlockSpec((1, t, LANES), lambda bi, i: (bi, 0, 0))],
        out_specs=pl.BlockSpec((1, blk, w), lambda bi, i: (bi, i, 0)),
        out_shape=jax.ShapeDtypeStruct((b, t, w), BF16),
        compiler_params=_params(("arbitrary", "arbitrary")),
        name="fox_prompt",
    )(q, k, v, bias_q, bias_k)


def _diff_lambda(lam_ref, lambda_init):
    lv = lam_ref[...]
    s1 = jnp.sum(lv[0:1] * lv[1:2], axis=-1, keepdims=True)
    s2 = jnp.sum(lv[2:3] * lv[3:4], axis=-1, keepdims=True)
    return jnp.exp(s1) - jnp.exp(s2) + lambda_init


def _subln(o, gain, lambda_init):
    ms = jnp.mean(o * o, axis=-1, keepdims=True)
    return o * lax.rsqrt(ms + LN_EPS) * gain * (1.0 - lambda_init)


def _alibi_slope(hd):
    return 2.0 ** (-8.0 * (hd + 1) / H_DIFF)


def _alibi_key_lanes(t):
    s = np.arange(t)
    lanes = np.zeros((t, LANES), np.float32)
    lanes[:, 0], lanes[:, 1], lanes[:, 2], lanes[:, 3] = s >> 8, s & 255, 1.0, 1.0
    return lanes


def _diff_prompt_kernel(q_ref, k_ref, v_ref, pos_ref, lam_ref, gain_ref, o_ref, *, blk, lambda_init,
                        heads_per_loop):
    i = pl.program_id(1)
    lane = lax.broadcasted_iota(jnp.int32, (blk, LANES), 1)
    lo = lane < HEAD_DIM
    ri = lax.broadcasted_iota(jnp.int32, (blk, blk), 0)
    ci = lax.broadcasted_iota(jnp.int32, (blk, blk), 1)
    causal = ci <= ri
    lam = _diff_lambda(lam_ref, lambda_init)
    t = i * blk + lax.broadcasted_iota(jnp.int32, (blk, LANES), 0)
    t_hi = (t >> 8).astype(F32)
    t_lo = (t & 255).astype(F32)

    def head_queries(hd):
        slope = _alibi_slope(hd)
        pos = jnp.where(lane == 0, 256.0 * slope,
                        jnp.where(lane == 1, slope,
                                  jnp.where(lane == 2, (-256.0 * slope) * t_hi,
                                            jnp.where(lane == 3, -slope * t_lo, 0.0))))
        q2 = q_ref[0, :, hd * LANES:(hd + 1) * LANES].astype(F32)
        return (jnp.concatenate([jnp.where(lo, q2, 0.0), pos], axis=1).astype(BF16),
                jnp.concatenate([jnp.where(lo, 0.0, q2), pos], axis=1).astype(BF16))

    for first in range(0, H_DIFF, heads_per_loop):
        heads = list(range(first, first + heads_per_loop))
        qmaps = [head_queries(hd) for hd in heads]

        def step(j, carry, masked, heads=heads, qmaps=qmaps):
            start = pl.multiple_of(j * blk, blk)
            pos_k = pos_ref[pl.ds(start, blk), :]
            out = []
            for hd, (q_1, q_2), (c1, c2) in zip(heads, qmaps, carry):
                cols = slice(hd * LANES, (hd + 1) * LANES)
                ks = jnp.concatenate([k_ref[0, pl.ds(start, blk), cols], pos_k], axis=1)
                vs = v_ref[0, pl.ds(start, blk), cols]
                s1 = lax.dot_general(q_1, ks, _NT, preferred_element_type=F32)
                s2 = lax.dot_general(q_2, ks, _NT, preferred_element_type=F32)
                if masked:
                    s1 = jnp.where(causal, s1, NEG_INF)
                    s2 = jnp.where(causal, s2, NEG_INF)
                out.append((_online_update(s1, *c1, vs), _online_update(s2, *c2, vs)))
            return tuple(out)

        init = tuple((_softmax_init(blk, LANES), _softmax_init(blk, LANES)) for _ in heads)
        carry = lax.fori_loop(0, i, functools.partial(step, masked=False), init)
        final = step(i, carry, True)
        for hd, ((_, l1, a1), (_, l2, a2)) in zip(heads, final):
            o = a1 / l1 - lam * (a2 / l2)
            o_ref[0, :, hd * LANES:(hd + 1) * LANES] = _subln(o, gain_ref[...], lambda_init).astype(BF16)


def _diff_prompt(q, k, v, lam_vecs, gain, *, blk, lambda_init):
    b, t, w = q.shape
    pos_k = jnp.asarray(_alibi_key_lanes(t), dtype=BF16)
    return pl.pallas_call(
        functools.partial(_diff_prompt_kernel, blk=blk, lambda_init=lambda_init,
                          heads_per_loop=DIFF_HEADS_PER_LOOP),
        grid=(b, t // blk),
        in_specs=[pl.BlockSpec((1, blk, w), lambda bi, i: (bi, i, 0)),
                  pl.BlockSpec((1, t, w), lambda bi, i: (bi, 0, 0)),
                  pl.BlockSpec((1, t, w), lambda bi, i: (bi, 0, 0)),
                  pl.BlockSpec(pos_k.shape, lambda bi, i: (0, 0)),
                  pl.BlockSpec(lam_vecs.shape, lambda bi, i: (0, 0)),
                  pl.BlockSpec(gain.shape, lambda bi, i: (0, 0))],
        out_specs=pl.BlockSpec((1, blk, w), lambda bi, i: (bi, i, 0)),
        out_shape=jax.ShapeDtypeStruct((b, t, w), BF16),
        compiler_params=_params(("arbitrary", "arbitrary")),
        name="diff_prompt",
    )(q, k, v, pos_k, lam_vecs, gain)


def _suffix_weights(page):
    t_in = np.arange(page)[:, None]
    suffix = t_in > np.arange(page)[None, :]
    return np.concatenate([suffix, np.ones_like(suffix)], axis=1).astype(np.float32)


def _suffix_sums(page_refs, w_ref, out_scr):
    n_pages = len(page_refs)
    hf, page = page_refs[0].shape
    lrow = jnp.concatenate([pr[...] for pr in page_refs], axis=0)
    wm = w_ref[...]
    y = sum(jnp.dot(piece, wm, preferred_element_type=F32) for piece in _split3(lrow))
    off = jnp.zeros((hf, page), F32)
    for p in reversed(range(n_pages)):
        out_scr[p] = y[p * hf:(p + 1) * hf, :page] + off
        off = off + y[p * hf:(p + 1) * hf, page:]


def _block_diag_queries(q):
    nq, w = q.shape
    groups = w // HEAD_DIM
    tiled = jnp.broadcast_to(q.astype(F32)[None], (groups, nq, w)).reshape(groups * nq, w)
    ri = lax.broadcasted_iota(jnp.int32, (groups * nq, w), 0)
    ci = lax.broadcasted_iota(jnp.int32, (groups * nq, w), 1)
    return jnp.where((ri >> _log2(nq)) == (ci >> _log2(HEAD_DIM)), tiled, 0.0).astype(BF16)


def _softmax_parts(scores):
    m = functools.reduce(jnp.maximum, [jnp.max(s, axis=-1, keepdims=True) for s in scores])
    ps = [jnp.exp(s - m) for s in scores]
    l = functools.reduce(jnp.add, [jnp.sum(p, axis=-1, keepdims=True) for p in ps])
    return ps, l


def _decode_kernel(pt_ref, *refs, n_pages, lambda_init):
    kf_refs = refs[0:n_pages]
    vf_refs = refs[n_pages:2 * n_pages]
    kd_refs = refs[2 * n_pages:3 * n_pages]
    vd_refs = refs[3 * n_pages:4 * n_pages]
    lf_refs = refs[4 * n_pages:5 * n_pages]
    (w_ref, qa_ref, ka_ref, va_ref, qb_ref, kb_ref, vb_ref, cq_ref, ckn_ref, lam_ref, gain_ref,
     oa_ref, ob_ref, r_scr) = refs[5 * n_pages:]
    nq = qa_ref.shape[1]
    page = kf_refs[0].shape[-1]
    rows = H_FOX * nq
    past_len = n_pages * page

    _suffix_sums(lf_refs, w_ref, r_scr)
    qf = _block_diag_queries(qa_ref[0])
    qd = _block_diag_queries(qb_ref[0])

    def pages_t(page_refs):
        return jnp.concatenate([pr[...].reshape(W_ATT, page) for pr in page_refs], axis=1).astype(BF16)

    pad = jnp.zeros((LANES - nq, W_ATT), F32)

    def padded(ref):
        return jnp.concatenate([ref[0], pad], axis=0).astype(BF16)

    row = lax.broadcasted_iota(jnp.int32, (rows, 1), 0)
    q_of_row = (row & (nq - 1)).astype(F32)
    slope_row = jnp.zeros((rows, 1), F32)
    for hd in range(H_DIFF):
        slope_row = jnp.where((row >> _log2(2 * nq)) == hd, _alibi_slope(hd), slope_row)
    col = lax.broadcasted_iota(jnp.int32, (rows, LANES), 1)
    visible = col <= (row & (nq - 1))

    s_past = jnp.dot(qf, pages_t(kf_refs), preferred_element_type=F32)
    r_tile = jnp.concatenate(
        [jnp.concatenate([jnp.broadcast_to(r_scr[p, hd:hd + 1, :], (nq, page)) for p in range(n_pages)],
                         axis=1) for hd in range(H_FOX)], axis=0)
    s_past = s_past + (cq_ref[0] + r_tile)
    s_new = lax.dot_general(qf, padded(ka_ref), _NT, preferred_element_type=F32)
    s_new = jnp.where(visible, s_new + (cq_ref[0] - ckn_ref[0]), NEG_INF)
    (p_past, p_new), l = _softmax_parts([s_past, s_new])
    acc = (lax.dot_general(p_past.astype(BF16), pages_t(vf_refs), _NT, preferred_element_type=F32)
           + jnp.dot(p_new.astype(BF16), padded(va_ref), preferred_element_type=F32))
    of = acc / l
    lane = lax.broadcasted_iota(jnp.int32, (nq, W_ATT), 1)
    oa = jnp.zeros((nq, W_ATT), F32)
    for hd in range(H_FOX):
        oa = jnp.where((lane >> _log2(HEAD_DIM)) == hd, of[hd * nq:(hd + 1) * nq, :], oa)
    oa_ref[0] = oa.astype(BF16)

    s_past = jnp.dot(qd, pages_t(kd_refs), preferred_element_type=F32)
    key_pos = lax.broadcasted_iota(jnp.int32, (rows, past_len), 1).astype(F32)
    s_past = s_past + slope_row * (key_pos - (past_len + q_of_row))
    s_new = lax.dot_general(qd, padded(kb_ref), _NT, preferred_element_type=F32)
    s_new = jnp.where(visible, s_new + slope_row * (col.astype(F32) - q_of_row), NEG_INF)
    (p_past, p_new), l = _softmax_parts([s_past, s_new])
    vb_new = padded(vb_ref)
    per_head = 2 * nq
    heads = []
    for hd in range(H_DIFF):
        rs = slice(hd * per_head, (hd + 1) * per_head)
        v_past = jnp.concatenate([pr[pl.ds(hd, page, stride=H_DIFF), :] for pr in vd_refs],
                                 axis=0).astype(BF16)
        heads.append(jnp.dot(p_past[rs].astype(BF16), v_past, preferred_element_type=F32)
                     + jnp.dot(p_new[rs].astype(BF16), vb_new[:, hd * V_DIFF:(hd + 1) * V_DIFF],
                               preferred_element_type=F32))
    od = jnp.concatenate(heads, axis=0) / l
    lam = _diff_lambda(lam_ref, lambda_init)
    outs = []
    for hd in range(H_DIFF):
        r0 = hd * per_head
        o = od[r0:r0 + nq] - lam * od[r0 + nq:r0 + per_head]
        outs.append(_subln(o, gain_ref[...], lambda_init))
    ob_ref[0] = jnp.concatenate(outs, axis=1).astype(BF16)


def _decode_attention(page_table, caches_t, logf_t, layer, qa, ka, va, qb, kb, vb, cq_col, ck_new,
                      lam_vecs, gain, *, lambda_init):
    n_seq, nq, w = qa.shape
    n_pages = page_table.shape[1]
    hf, page = logf_t.shape[2:]
    wmat = jnp.asarray(_suffix_weights(page), dtype=BF16)

    def page_spec(cache, p):
        tail = cache.shape[2:]
        return pl.BlockSpec((None, None) + tail,
                            functools.partial(lambda b, pt, p: (layer, pt[b, p]) + (0,) * len(tail), p=p))

    def per_seq(shape):
        nd = len(shape)
        return pl.BlockSpec((1,) + tuple(shape[1:]), lambda b, pt: (b,) + (0,) * (nd - 1))

    paged = list(caches_t) + [logf_t]
    cache_specs = [page_spec(c, p) for c in paged for p in range(n_pages)]
    cache_args = [c for c in paged for _ in range(n_pages)]
    small = [qa, ka, va, qb, kb, vb, cq_col, ck_new]
    grid_spec = pltpu.PrefetchScalarGridSpec(
        num_scalar_prefetch=1,
        grid=(n_seq,),
        in_specs=cache_specs + [pl.BlockSpec(wmat.shape, lambda b, pt: (0, 0))]
        + [per_seq(a.shape) for a in small]
        + [pl.BlockSpec(lam_vecs.shape, lambda b, pt: (0, 0)),
           pl.BlockSpec(gain.shape, lambda b, pt: (0, 0))],
        out_specs=[per_seq(qa.shape), per_seq(qa.shape)],
        scratch_shapes=[pltpu.VMEM((n_pages, hf, page), F32)],
    )
    return pl.pallas_call(
        functools.partial(_decode_kernel, n_pages=n_pages, lambda_init=lambda_init),
        grid_spec=grid_spec,
        out_shape=[jax.ShapeDtypeStruct(qa.shape, BF16)] * 2,
        compiler_params=_params(("arbitrary",)),
        name="decode_attention",
    )(page_table, *cache_args, wmat, *small, lam_vecs, gain)


def _layer_norm(x, g, b):
    mu = jnp.mean(x, axis=-1, keepdims=True)
    xc = x - mu
    var = jnp.mean(xc * xc, axis=-1, keepdims=True)
    return xc * lax.rsqrt(var + LN_EPS) * g + b


def _ffn_kernel(x_ref, oa_ref, ob_ref, ga_ref, gb_ref, g1_ref, sh2_ref, sc2_ref, g2_ref,
                wa_ref, wb_ref, wo_ref, ln1g_ref, ln1b_ref, wg_ref, wu_ref, wd_ref,
                ln2g_ref, ln2b_ref, y_ref, *, alpha):
    nb, r, d = x_ref.shape
    rows = nb * r

    def flat(ref):
        return ref[...].reshape(rows, ref.shape[-1])

    def per_seq(ref):
        return jnp.broadcast_to(ref[...], (nb, r, d)).reshape(rows, d)

    a = jnp.dot(flat(oa_ref), wa_ref[...], preferred_element_type=F32)
    b = jnp.dot(flat(ob_ref), wb_ref[...], preferred_element_type=F32)
    m = flat(ga_ref).astype(F32) * a + flat(gb_ref).astype(F32) * b
    mo = jnp.dot(m.astype(BF16), wo_ref[...], preferred_element_type=F32)
    x1 = _layer_norm(alpha * flat(x_ref) + per_seq(g1_ref) * mo, ln1g_ref[...], ln1b_ref[...])
    h2 = (x1 * (1.0 + per_seq(sc2_ref)) + per_seq(sh2_ref)).astype(BF16)
    g = jnp.dot(h2, wg_ref[...], preferred_element_type=F32)
    u = jnp.dot(h2, wu_ref[...], preferred_element_type=F32)
    act = (g * jax.nn.sigmoid(g) * u).astype(BF16)
    f = jnp.dot(act, wd_ref[...], preferred_element_type=F32)
    y = _layer_norm(alpha * x1 + per_seq(g2_ref) * f, ln2g_ref[...], ln2b_ref[...])
    y_ref[...] = y.reshape(nb, r, d)


def _merge_ffn(x, oa, ob, ga, gb, g1, sh2, sc2, g2, weights, *, seqs_per_block, rows_per_seq_block,
               alpha):
    n_seq, t, d = x.shape
    nb, r = seqs_per_block, rows_per_seq_block

    def blk(width):
        return pl.BlockSpec((nb, r, width), lambda i, j: (i, j, 0))

    mod = pl.BlockSpec((nb, 1, d), lambda i, j: (i, 0, 0))
    return pl.pallas_call(
        functools.partial(_ffn_kernel, alpha=alpha),
        grid=(n_seq // nb, t // r),
        in_specs=[blk(d), blk(W_ATT), blk(W_ATT), blk(d), blk(d), mod, mod, mod, mod]
        + [_resident(wt.shape) for wt in weights],
        out_specs=blk(d),
        out_shape=jax.ShapeDtypeStruct((n_seq, t, d), F32),
        compiler_params=_params(("arbitrary", "arbitrary")),
        name="merge_ffn",
    )(x, oa, ob, ga, gb, g1, sh2, sc2, g2, *weights)


def kernel(x_prompt, x_sample, c_prompt, c_sample, cache_k_fox, cache_v_fox, cache_logf_fox,
           cache_k_diff, cache_v_diff, page_table, w_ada, b_ada, w_in, b_forget,
           lambda_q1, lambda_k1, lambda_q2, lambda_k2, subln_gain, w_branch_a, w_branch_b,
           w_out, ln1_gain, ln1_bias, w_ffn_gate, w_ffn_up, w_ffn_down, ln2_gain, ln2_bias):
    depth = w_in.shape[0]
    batch, seq, d = x_prompt.shape
    dec_batch, dec_seq, _ = x_sample.shape
    alpha = (2 * depth) ** 0.25
    w = W_ATT
    f_start = 3 * w
    caches_t = (jnp.transpose(cache_k_fox, (0, 1, 3, 4, 2)), jnp.transpose(cache_v_fox, (0, 1, 3, 4, 2)),
                jnp.transpose(cache_k_diff, (0, 1, 3, 4, 5, 2)),
                cache_v_diff.reshape(cache_v_diff.shape[:2] + (-1, V_DIFF)))
    logf_t = jnp.transpose(cache_logf_fox, (0, 1, 3, 2))

    xp, xs = x_prompt, x_sample
    outs_p = [[] for _ in range(5)]
    outs_s = [[] for _ in range(5)]
    for l in range(depth):
        lambda_init = 0.8 - 0.6 * math.exp(-0.3 * l)
        w_in_r = jnp.concatenate(
            [w_in[l][:, :f_start], w_in[l][:, f_start + H_FOX:], w_in[l][:, f_start:f_start + H_FOX],
             jnp.zeros((d, LANES - H_FOX), F32)], axis=1).astype(BF16)
        bf_pad = jnp.concatenate([b_forget[l], jnp.zeros((LANES - H_FOX,), F32)]).reshape(1, LANES)
        lam_vecs = jnp.stack([lambda_q1[l], lambda_k1[l], lambda_q2[l], lambda_k2[l]]).astype(F32)
        gain = subln_gain[l].reshape(1, V_DIFF)
        ffn_w = (w_branch_a[l].astype(BF16), w_branch_b[l].astype(BF16), w_out[l].astype(BF16),
                 ln1_gain[l].reshape(1, d), ln1_bias[l].reshape(1, d),
                 w_ffn_gate[l].astype(BF16), w_ffn_up[l].astype(BF16), w_ffn_down[l].astype(BF16),
                 ln2_gain[l].reshape(1, d), ln2_bias[l].reshape(1, d))

        mod = _adaln(jnp.concatenate([c_prompt, c_sample], axis=0), w_ada[l], b_ada[l])
        mod_p = [mod[k, :batch].reshape(batch, 1, d) for k in range(6)]
        mod_s = [mod[k, batch:].reshape(dec_batch, 1, d) for k in range(6)]

        sh1, sc1, g1, sh2, sc2, g2 = mod_p
        (qa, ka, va, qb, kb, vb, ga, gb, logf, ka_b, va_b, kb_b, vb_b, bias_q, bias_k) = _project(
            xp, sc1, sh1, w_in_r, bf_pad, seqs_per_block=1, rows_per_seq_block=PROJ_ROWS,
            group=seq, prompt=True)
        oa = _fox_prompt(qa, ka_b, va_b, bias_q, bias_k, blk=ATT_BLOCK)
        ob = _diff_prompt(qb, kb_b, vb_b, lam_vecs, gain, blk=ATT_BLOCK, lambda_init=lambda_init)
        xp = _merge_ffn(xp, oa, ob, ga, gb, g1, sh2, sc2, g2, ffn_w,
                        seqs_per_block=1, rows_per_seq_block=FFN_ROWS, alpha=alpha)
        for lst, val in zip(outs_p, (ka.reshape(batch, H_FOX, HEAD_DIM, seq).transpose(0, 3, 1, 2),
                                     va.reshape(batch, H_FOX, HEAD_DIM, seq).transpose(0, 3, 1, 2), logf,
                                     kb.reshape(batch, H_DIFF, 2, HEAD_DIM, seq).transpose(0, 4, 1, 2, 3),
                                     vb.reshape(batch, seq, H_DIFF, V_DIFF))):
            lst.append(val)

        sh1, sc1, g1, sh2, sc2, g2 = mod_s
        seqs_blk = PROJ_ROWS // dec_seq
        (qa, ka, va, qb, kb, vb, ga, gb, logf, cum) = _project(
            xs, sc1, sh1, w_in_r, bf_pad, seqs_per_block=seqs_blk, rows_per_seq_block=dec_seq,
            group=dec_seq, prompt=False)
        cum_hq = cum.transpose(0, 2, 1)
        cq_col = cum_hq.reshape(dec_batch, H_FOX * dec_seq, 1)
        ck_new = jnp.repeat(cum_hq, dec_seq, axis=1)
        ck_new = jnp.pad(ck_new, ((0, 0), (0, 0), (0, LANES - dec_seq)))
        oa, ob = _decode_attention(page_table, caches_t, logf_t, l, qa, ka, va, qb, kb, vb, cq_col, ck_new,
                                   lam_vecs, gain, lambda_init=lambda_init)
        ffn_seqs = FFN_ROWS // dec_seq
        xs = _merge_ffn(xs, oa, ob, ga, gb, g1, sh2, sc2, g2, ffn_w,
                        seqs_per_block=ffn_seqs, rows_per_seq_block=dec_seq, alpha=alpha)
        for lst, val in zip(outs_s, (ka.reshape(dec_batch, dec_seq, H_FOX, HEAD_DIM),
                                     va.reshape(dec_batch, dec_seq, H_FOX, HEAD_DIM), logf,
                                     kb.reshape(dec_batch, dec_seq, H_DIFF, 2, HEAD_DIM),
                                     vb.reshape(dec_batch, dec_seq, H_DIFF, V_DIFF))):
            lst.append(val)

    return (xp, xs, *[jnp.stack(o, 0) for o in outs_p], *[jnp.stack(o, 0) for o in outs_s])
```

```python
import functools
import math

import jax
import jax.numpy as jnp
import numpy as np
from jax import lax
from jax.experimental import pallas as pl
from jax.experimental.pallas import tpu as pltpu

F32 = jnp.float32
BF16 = jnp.bfloat16

LANES = 128
HEAD_DIM = 64
H_FOX = 8
H_DIFF = 4
V_DIFF = 2 * HEAD_DIM
W_ATT = H_FOX * HEAD_DIM
LN_EPS = 1e-5
NEG_INF = -1e30
LOG2E = math.log2(math.e)
VMEM_LIMIT = 56 * 1024 * 1024

PROJ_ROWS = 512
ATT_BLOCK = 512
FOX_HEADS_PER_LOOP = 4
DIFF_HEADS_PER_LOOP = 2
BIAS_LANES = 8
FFN_ROWS = 512

_NT = (((1,), (1,)), ((), ()))


def _log2(n):
    assert n & (n - 1) == 0, n
    return n.bit_length() - 1


def _resident(shape):
    nd = len(shape)
    return pl.BlockSpec(shape, lambda *_: (0,) * nd, pipeline_mode=pl.Buffered(1))


def _params(sem):
    return pltpu.CompilerParams(dimension_semantics=sem, vmem_limit_bytes=VMEM_LIMIT)


def _split3(x):
    hi = x.astype(BF16)
    r1 = x - hi.astype(F32)
    mid = r1.astype(BF16)
    lo = (r1 - mid.astype(F32)).astype(BF16)
    return hi, mid, lo


def _ada_kernel(c_ref, w_ref, b_ref, o_ref):
    o_ref[0] = jnp.dot(c_ref[...].astype(BF16), w_ref[...].astype(BF16),
                       preferred_element_type=F32) + b_ref[...]


def _adaln(c_all, w_ada, b_ada):
    m, d = c_all.shape
    n_chunks = w_ada.shape[1] // d
    return pl.pallas_call(
        _ada_kernel,
        grid=(n_chunks,),
        in_specs=[pl.BlockSpec((m, d), lambda j: (0, 0)),
                  pl.BlockSpec((d, d), lambda j: (0, j)),
                  pl.BlockSpec((1, d), lambda j: (0, j))],
        out_specs=pl.BlockSpec((1, m, d), lambda j: (j, 0, 0)),
        out_shape=jax.ShapeDtypeStruct((n_chunks, m, d), F32),
        compiler_params=_params(("arbitrary",)),
        name="adaln",
    )(c_all, w_ada, b_ada.reshape(1, -1))


def _fox_bias_map():
    e = np.zeros((3 * LANES, 2 * LANES), np.float32)
    ones = np.zeros((1, 2 * LANES), np.float32)
    for hd in range(H_FOX):
        for k in range(3):
            e[LANES * k + hd, BIAS_LANES * hd + k] = -1.0
            e[LANES * k + hd, LANES + BIAS_LANES * hd + 3 + k] = 1.0
            ones[0, BIAS_LANES * hd + 3 + k] = 1.0
            ones[0, LANES + BIAS_LANES * hd + k] = 1.0
    return e, ones


def _group_cumsum(x, group):
    rows = x.shape[0]
    tile = min(rows, LANES)
    assert group <= tile or group >= rows, (group, rows)
    ri = lax.broadcasted_iota(jnp.int32, (tile, tile), 0)
    ci = lax.broadcasted_iota(jnp.int32, (tile, tile), 1)
    gs = _log2(min(group, tile))
    tri = jnp.where((ci <= ri) & ((ri >> gs) == (ci >> gs)), 1.0, 0.0).astype(BF16)
    out, running = [], jnp.zeros((1, x.shape[1]), F32)
    for k in range(rows // tile):
        c = sum(jnp.dot(tri, piece, preferred_element_type=F32)
                for piece in _split3(x[k * tile:(k + 1) * tile]))
        if group >= rows:
            c = c + running
            running = c[tile - 1:tile]
        out.append(c)
    return jnp.concatenate(out, axis=0)


def _proj_kernel(x_ref, sc_ref, sh_ref, w_ref, bf_ref, *rest, group, prompt, fox_q_scale):
    if prompt:
        (emap_ref, eones_ref, qa_ref, ka_ref, va_ref, qb_ref, kb_ref, vb_ref, ga_ref, gb_ref, logf_ref,
         kab_ref, vab_ref, kbb_ref, vbb_ref, bq_ref, bk_ref, carry_ref) = rest
    else:
        (qa_ref, ka_ref, va_ref, qb_ref, kb_ref, vb_ref, ga_ref, gb_ref, logf_ref, cum_ref) = rest
    nb, r, d = x_ref.shape
    rows = nb * r
    h = x_ref[...] * (1.0 + sc_ref[...]) + sh_ref[...]
    hb = h.reshape(rows, d).astype(BF16)

    def col(start, width):
        return jnp.dot(hb, w_ref[:, start:start + width], preferred_element_type=F32)

    scale = HEAD_DIM ** -0.5
    w = W_ATT
    qa_ref[...] = (col(0, w) * fox_q_scale).astype(BF16).reshape(nb, r, w)
    ka = col(w, w)
    va = col(2 * w, w)
    qb_ref[...] = (col(3 * w, w) * scale).astype(BF16).reshape(nb, r, w)
    kb = col(4 * w, w)
    vb = col(5 * w, w)
    if prompt:
        ka_ref[0] = ka.T
        va_ref[0] = va.T
        kb_ref[0] = kb.T
        for hd in range(H_DIFF):
            vb_ref[0, pl.ds(hd, rows, stride=H_DIFF), :] = vb[:, hd * V_DIFF:(hd + 1) * V_DIFF]
        kab_ref[...] = ka.astype(BF16).reshape(nb, r, w)
        vab_ref[...] = va.astype(BF16).reshape(nb, r, w)
        kbb_ref[...] = kb.astype(BF16).reshape(nb, r, w)
        vbb_ref[...] = vb.astype(BF16).reshape(nb, r, w)
    else:
        ka_ref[...] = ka.reshape(nb, r, w)
        va_ref[...] = va.reshape(nb, r, w)
        kb_ref[...] = kb.reshape(nb, r, w)
        vb_ref[...] = vb.reshape(nb, r, w)
    ga_ref[...] = jax.nn.sigmoid(col(6 * w, d)).astype(BF16).reshape(nb, r, d)
    gb_ref[...] = jax.nn.sigmoid(col(6 * w + d, d)).astype(BF16).reshape(nb, r, d)

    fa = col(6 * w + 2 * d, LANES) + bf_ref[...]
    logf = jax.nn.log_sigmoid(fa)
    logf_ref[...] = logf[:, :H_FOX].reshape(nb, r, H_FOX)
    cum = _group_cumsum(logf, group)
    if not prompt:
        cum_ref[...] = cum[:, :H_FOX].reshape(nb, r, H_FOX)
        return
    @pl.when(pl.program_id(1) == 0)
    def _():
        carry_ref[...] = jnp.zeros_like(carry_ref)
    cum = cum + carry_ref[...]
    carry_ref[...] = cum[rows - 1:rows, :]
    pieces = jnp.concatenate(_split3(cum * LOG2E), axis=1)
    placed = jnp.dot(pieces, emap_ref[...], preferred_element_type=F32) + eones_ref[...]
    bk_ref[...] = placed[:, :LANES].astype(BF16).reshape(nb, r, LANES)
    bq_ref[...] = placed[:, LANES:].astype(BF16).reshape(nb, r, LANES)


def _project(x, sc, sh, w_in_r, bf_pad, *, seqs_per_block, rows_per_seq_block, group, prompt):
    n_seq, t, d = x.shape
    nb, r = seqs_per_block, rows_per_seq_block
    grid = (n_seq // nb, t // r)
    w = W_ATT

    def blk(width):
        return pl.BlockSpec((nb, r, width), lambda i, j: (i, j, 0))

    def out(width, dtype):
        return jax.ShapeDtypeStruct((n_seq, t, width), dtype)

    mod = pl.BlockSpec((nb, 1, d), lambda i, j: (i, 0, 0))
    if prompt:
        kv_t = jax.ShapeDtypeStruct((n_seq, w, t), F32)
        kv_t_spec = pl.BlockSpec((nb, w, r), lambda i, j: (i, 0, j))
        vb_rows = jax.ShapeDtypeStruct((n_seq, t * H_DIFF, V_DIFF), F32)
        vb_rows_spec = pl.BlockSpec((nb, r * H_DIFF, V_DIFF), lambda i, j: (i, j, 0))
        kv_shapes = [kv_t, kv_t, kv_t, vb_rows]
        kv_specs = [kv_t_spec, kv_t_spec, kv_t_spec, vb_rows_spec]
    else:
        kv_shapes = [out(w, F32)] * 4
        kv_specs = [blk(w)] * 4
    out_shape = [out(w, BF16), kv_shapes[0], kv_shapes[1], out(w, BF16), kv_shapes[2], kv_shapes[3],
                 out(d, BF16), out(d, BF16), out(H_FOX, F32)]
    out_specs = [blk(w), kv_specs[0], kv_specs[1], blk(w), kv_specs[2], kv_specs[3],
                 blk(d), blk(d), blk(H_FOX)]
    inputs = [x, sc, sh, w_in_r, bf_pad]
    in_specs = [blk(d), mod, mod, _resident(w_in_r.shape), _resident(bf_pad.shape)]
    scratch = []
    if prompt:
        emap, eones = _fox_bias_map()
        inputs += [jnp.asarray(emap, dtype=BF16), jnp.asarray(eones)]
        in_specs += [_resident(emap.shape), _resident(eones.shape)]
        out_shape += [out(w, BF16)] * 4 + [out(LANES, BF16)] * 2
        out_specs += [blk(w)] * 4 + [blk(LANES)] * 2
        scratch = [pltpu.VMEM((1, LANES), F32)]
    else:
        out_shape += [out(H_FOX, F32)]
        out_specs += [blk(H_FOX)]
    return pl.pallas_call(
        functools.partial(_proj_kernel, group=group, prompt=prompt,
                          fox_q_scale=HEAD_DIM ** -0.5 * (LOG2E if prompt else 1.0)),
        grid=grid,
        in_specs=in_specs,
        out_specs=out_specs,
        out_shape=out_shape,
        scratch_shapes=scratch,
        compiler_params=_params(("arbitrary", "arbitrary")),
        name="project",
    )(*inputs)


def _online_update(s, m, l, acc, v):
    m_new = jnp.maximum(m, jnp.max(s, axis=-1, keepdims=True))
    alpha = jnp.exp(m - m_new)
    p = jnp.exp(s - m_new)
    l_new = alpha * l + jnp.sum(p, axis=-1, keepdims=True)
    acc_new = alpha * acc + jnp.dot(p.astype(BF16), v, preferred_element_type=F32)
    return m_new, l_new, acc_new


def _softmax_init(rows, width):
    return (jnp.full((rows, 1), NEG_INF, F32), jnp.zeros((rows, 1), F32), jnp.zeros((rows, width), F32))


def _online_update2(s, m, l, acc, v):
    m_new = jnp.maximum(m, jnp.max(s, axis=-1, keepdims=True))
    alpha = jnp.exp2(m - m_new)
    p = jnp.exp2(s - m_new)
    l_new = alpha * l + jnp.sum(p, axis=-1, keepdims=True)
    acc_new = alpha * acc + jnp.dot(p.astype(BF16), v, preferred_element_type=F32)
    return m_new, l_new, acc_new


def _fox_prompt_kernel(q_ref, k_ref, v_ref, bq_ref, ak_ref, o_ref, *, blk, heads_per_loop):
    i = pl.program_id(1)
    lane = lax.broadcasted_iota(jnp.int32, (blk, LANES), 1)
    lo = lane < HEAD_DIM
    ri = lax.broadcasted_iota(jnp.int32, (blk, blk), 0)
    ci = lax.broadcasted_iota(jnp.int32, (blk, blk), 1)
    causal = ci <= ri
    bq = bq_ref[0].astype(F32)
    lane_row = lax.broadcasted_iota(jnp.int32, (1, LANES), 1)
    keep_lanes = (jnp.where(lane_row < HEAD_DIM, 1.0, 0.0).astype(BF16),
                  jnp.where(lane_row < HEAD_DIM, 0.0, 1.0).astype(BF16))

    def pair_cols(hd):
        return slice((hd // 2) * LANES, (hd // 2 + 1) * LANES)

    def head_query(hd):
        q2 = q_ref[0, :, pair_cols(hd)].astype(F32)
        keep = lo if hd % 2 == 0 else jnp.logical_not(lo)
        own = (lane >> _log2(BIAS_LANES)) == hd
        return jnp.concatenate([jnp.where(keep, q2, 0.0), jnp.where(own, bq, 0.0)],
                               axis=1).astype(BF16)

    for first in range(0, H_FOX, heads_per_loop):
        heads = list(range(first, first + heads_per_loop))
        qs = [head_query(hd) for hd in heads]

        def step(j, carry, masked, heads=heads, qs=qs):
            start = pl.multiple_of(j * blk, blk)
            a_blk = ak_ref[0, pl.ds(start, blk), :]
            out = []
            for hd, qh, (m, acc) in zip(heads, qs, carry):
                ks = jnp.concatenate([k_ref[0, pl.ds(start, blk), pair_cols(hd)], a_blk], axis=1)
                vs = v_ref[0, pl.ds(start, blk), pair_cols(hd)] * keep_lanes[hd % 2] + keep_lanes[1 - hd % 2]
                s = lax.dot_general(qh, ks, _NT, preferred_element_type=F32)
                if masked:
                    s = jnp.where(causal, s, NEG_INF)
                m_new = jnp.maximum(m, jnp.max(s, axis=-1, keepdims=True))
                p = jnp.exp2(s - m_new).astype(BF16)
                acc = jnp.exp2(m - m_new) * acc + jnp.dot(p, vs, preferred_element_type=F32)
                out.append((m_new, acc))
            return tuple(out)

        init = tuple((jnp.full((blk, 1), NEG_INF, F32), jnp.zeros((blk, LANES), F32)) for _ in heads)
        carry = lax.fori_loop(0, i, functools.partial(step, masked=False), init)
        final = step(i, carry, True)
        for k in range(0, heads_per_loop, 2):
            (_, a0), (_, a1) = final[k], final[k + 1]
            o_ref[0, :, pair_cols(first + k)] = jnp.where(
                lo, a0 / pltpu.roll(a0, HEAD_DIM, 1), a1 / pltpu.roll(a1, HEAD_DIM, 1)).astype(BF16)


def _fox_prompt(q, k, v, bias_q, bias_k, *, blk):
    b, t, w = q.shape
    return pl.pallas_call(
        functools.partial(_fox_prompt_kernel, blk=blk, heads_per_loop=FOX_HEADS_PER_LOOP),
        grid=(b, t // blk),
        in_specs=[pl.BlockSpec((1, blk, w), lambda bi, i: (bi, i, 0)),
                  pl.BlockSpec((1, t, w), lambda bi, i: (bi, 0, 0)),
                  pl.BlockSpec((1, t, w), lambda bi, i: (bi, 0, 0)),
                  pl.BlockSpec((1, blk, LANES), lambda bi, i: (bi, i, 0)),
                  pl.BlockSpec((1, t, LANES), lambda bi, i: (bi, 0, 0))],
        out_specs=pl.BlockSpec((1, blk, w), lambda bi, i: (bi, i, 0)),
        out_shape=jax.ShapeDtypeStruct((b, t, w), BF16),
        compiler_params=_params(("arbitrary", "arbitrary")),
        name="fox_prompt",
    )(q, k, v, bias_q, bias_k)


def _diff_lambda(lam_ref, lambda_init):
    lv = lam_ref[...]
    s1 = jnp.sum(lv[0:1] * lv[1:2], axis=-1, keepdims=True)
    s2 = jnp.sum(lv[2:3] * lv[3:4], axis=-1, keepdims=True)
    return jnp.exp(s1) - jnp.exp(s2) + lambda_init


def _subln(o, gain, lambda_init):
    ms = jnp.mean(o * o, axis=-1, keepdims=True)
    return o * lax.rsqrt(ms + LN_EPS) * gain * (1.0 - lambda_init)


def _alibi_slope(hd):
    return 2.0 ** (-8.0 * (hd + 1) / H_DIFF)


def _alibi_key_lanes(t):
    s = np.arange(t)
    lanes = np.zeros((t, LANES), np.float32)
    lanes[:, 0], lanes[:, 1], lanes[:, 2], lanes[:, 3] = s >> 8, s & 255, 1.0, 1.0
    return lanes


def _diff_prompt_kernel(q_ref, k_ref, v_ref, pos_ref, lam_ref, gain_ref, o_ref, *, blk, lambda_init,
                        heads_per_loop):
    i = pl.program_id(1)
    lane = lax.broadcasted_iota(jnp.int32, (blk, LANES), 1)
    lo = lane < HEAD_DIM
    ri = lax.broadcasted_iota(jnp.int32, (blk, blk), 0)
    ci = lax.broadcasted_iota(jnp.int32, (blk, blk), 1)
    causal = ci <= ri
    lam = _diff_lambda(lam_ref, lambda_init)
    t = i * blk + lax.broadcasted_iota(jnp.int32, (blk, LANES), 0)
    t_hi = (t >> 8).astype(F32)
    t_lo = (t & 255).astype(F32)

    def head_queries(hd):
        slope = _alibi_slope(hd)
        pos = jnp.where(lane == 0, 256.0 * slope,
                        jnp.where(lane == 1, slope,
                                  jnp.where(lane == 2, (-256.0 * slope) * t_hi,
                                            jnp.where(lane == 3, -slope * t_lo, 0.0))))
        q2 = q_ref[0, :, hd * LANES:(hd + 1) * LANES].astype(F32)
        return (jnp.concatenate([jnp.where(lo, q2, 0.0), pos], axis=1).astype(BF16),
                jnp.concatenate([jnp.where(lo, 0.0, q2), pos], axis=1).astype(BF16))

    ones_tile = jnp.ones((blk, LANES), BF16)

    def update(s, m, acc, vs):
        m_new = jnp.maximum(m, jnp.max(s, axis=-1, keepdims=True))
        p = jnp.exp(s - m_new).astype(BF16)
        return m_new, jnp.exp(m - m_new) * acc + jnp.dot(p, vs, preferred_element_type=F32)

    for first in range(0, H_DIFF, heads_per_loop):
        heads = list(range(first, first + heads_per_loop))
        qmaps = [head_queries(hd) for hd in heads]

        def step(j, carry, masked, heads=heads, qmaps=qmaps):
            start = pl.multiple_of(j * blk, blk)
            pos_k = pos_ref[pl.ds(start, blk), :]
            out = []
            for hd, (q_1, q_2), (c1, c2) in zip(heads, qmaps, carry):
                cols = slice(hd * LANES, (hd + 1) * LANES)
                ks = jnp.concatenate([k_ref[0, pl.ds(start, blk), cols], pos_k], axis=1)
                vs = jnp.concatenate([v_ref[0, pl.ds(start, blk), cols], ones_tile], axis=1)
                s1 = lax.dot_general(q_1, ks, _NT, preferred_element_type=F32)
                s2 = lax.dot_general(q_2, ks, _NT, preferred_element_type=F32)
                if masked:
                    s1 = jnp.where(causal, s1, NEG_INF)
                    s2 = jnp.where(causal, s2, NEG_INF)
                out.append((update(s1, *c1, vs), update(s2, *c2, vs)))
            return tuple(out)

        def chain_init():
            return jnp.full((blk, 1), NEG_INF, F32), jnp.zeros((blk, 2 * LANES), F32)

        init = tuple((chain_init(), chain_init()) for _ in heads)
        carry = lax.fori_loop(0, i, functools.partial(step, masked=False), init)
        final = step(i, carry, True)
        for hd, ((_, a1), (_, a2)) in zip(heads, final):
            o = a1[:, :LANES] / a1[:, LANES:] - lam * (a2[:, :LANES] / a2[:, LANES:])
            o_ref[0, :, hd * LANES:(hd + 1) * LANES] = _subln(o, gain_ref[...], lambda_init).astype(BF16)


def _diff_prompt(q, k, v, lam_vecs, gain, *, blk, lambda_init):
    b, t, w = q.shape
    pos_k = jnp.asarray(_alibi_key_lanes(t), dtype=BF16)
    return pl.pallas_call(
        functools.partial(_diff_prompt_kernel, blk=blk, lambda_init=lambda_init,
                          heads_per_loop=DIFF_HEADS_PER_LOOP),
        grid=(b, t // blk),
        in_specs=[pl.BlockSpec((1, blk, w), lambda bi, i: (bi, i, 0)),
                  pl.BlockSpec((1, t, w), lambda bi, i: (bi, 0, 0)),
                  pl.BlockSpec((1, t, w), lambda bi, i: (bi, 0, 0)),
                  pl.BlockSpec(pos_k.shape, lambda bi, i: (0, 0)),
                  pl.BlockSpec(lam_vecs.shape, lambda bi, i: (0, 0)),
                  pl.BlockSpec(gain.shape, lambda bi, i: (0, 0))],
        out_specs=pl.BlockSpec((1, blk, w), lambda bi, i: (bi, i, 0)),
        out_shape=jax.ShapeDtypeStruct((b, t, w), BF16),
        compiler_params=_params(("arbitrary", "arbitrary")),
        name="diff_prompt",
    )(q, k, v, pos_k, lam_vecs, gain)


def _suffix_weights(page):
    t_in = np.arange(page)[:, None]
    suffix = t_in > np.arange(page)[None, :]
    return np.concatenate([suffix, np.ones_like(suffix)], axis=1).astype(np.float32)


def _suffix_sums(page_refs, w_ref, out_scr):
    n_pages = len(page_refs)
    hf, page = page_refs[0].shape
    lrow = jnp.concatenate([pr[...] for pr in page_refs], axis=0)
    wm = w_ref[...]
    y = sum(jnp.dot(piece, wm, preferred_element_type=F32) for piece in _split3(lrow))
    off = jnp.zeros((hf, page), F32)
    for p in reversed(range(n_pages)):
        out_scr[p] = y[p * hf:(p + 1) * hf, :page] + off
        off = off + y[p * hf:(p + 1) * hf, page:]


def _block_diag_queries(q):
    nq, w = q.shape
    groups = w // HEAD_DIM
    tiled = jnp.broadcast_to(q.astype(F32)[None], (groups, nq, w)).reshape(groups * nq, w)
    ri = lax.broadcasted_iota(jnp.int32, (groups * nq, w), 0)
    ci = lax.broadcasted_iota(jnp.int32, (groups * nq, w), 1)
    return jnp.where((ri >> _log2(nq)) == (ci >> _log2(HEAD_DIM)), tiled, 0.0).astype(BF16)


def _softmax_parts(scores):
    m = functools.reduce(jnp.maximum, [jnp.max(s, axis=-1, keepdims=True) for s in scores])
    ps = [jnp.exp(s - m) for s in scores]
    l = functools.reduce(jnp.add, [jnp.sum(p, axis=-1, keepdims=True) for p in ps])
    return ps, l


def _decode_kernel(pt_ref, *refs, n_pages, lambda_init):
    kf_refs = refs[0:n_pages]
    vf_refs = refs[n_pages:2 * n_pages]
    kd_refs = refs[2 * n_pages:3 * n_pages]
    vd_refs = refs[3 * n_pages:4 * n_pages]
    lf_refs = refs[4 * n_pages:5 * n_pages]
    (w_ref, qa_ref, ka_ref, va_ref, qb_ref, kb_ref, vb_ref, cq_ref, ckn_ref, lam_ref, gain_ref,
     oa_ref, ob_ref, r_scr) = refs[5 * n_pages:]
    nq = qa_ref.shape[1]
    page = kf_refs[0].shape[-1]
    rows = H_FOX * nq
    past_len = n_pages * page

    _suffix_sums(lf_refs, w_ref, r_scr)
    qf = _block_diag_queries(qa_ref[0])
    qd = _block_diag_queries(qb_ref[0])

    def pages_t(page_refs):
        return jnp.concatenate([pr[...].reshape(W_ATT, page) for pr in page_refs], axis=1).astype(BF16)

    pad = jnp.zeros((LANES - nq, W_ATT), F32)

    def padded(ref):
        return jnp.concatenate([ref[0], pad], axis=0).astype(BF16)

    row = lax.broadcasted_iota(jnp.int32, (rows, 1), 0)
    q_of_row = (row & (nq - 1)).astype(F32)
    slope_row = jnp.zeros((rows, 1), F32)
    for hd in range(H_DIFF):
        slope_row = jnp.where((row >> _log2(2 * nq)) == hd, _alibi_slope(hd), slope_row)
    col = lax.broadcasted_iota(jnp.int32, (rows, LANES), 1)
    visible = col <= (row & (nq - 1))

    s_past = jnp.dot(qf, pages_t(kf_refs), preferred_element_type=F32)
    r_tile = jnp.concatenate(
        [jnp.concatenate([jnp.broadcast_to(r_scr[p, hd:hd + 1, :], (nq, page)) for p in range(n_pages)],
                         axis=1) for hd in range(H_FOX)], axis=0)
    s_past = s_past + (cq_ref[0] + r_tile)
    s_new = lax.dot_general(qf, padded(ka_ref), _NT, preferred_element_type=F32)
    s_new = jnp.where(visible, s_new + (cq_ref[0] - ckn_ref[0]), NEG_INF)
    (p_past, p_new), l = _softmax_parts([s_past, s_new])
    acc = (lax.dot_general(p_past.astype(BF16), pages_t(vf_refs), _NT, preferred_element_type=F32)
           + jnp.dot(p_new.astype(BF16), padded(va_ref), preferred_element_type=F32))
    of = acc / l
    lane = lax.broadcasted_iota(jnp.int32, (nq, W_ATT), 1)
    oa = jnp.zeros((nq, W_ATT), F32)
    for hd in range(H_FOX):
        oa = jnp.where((lane >> _log2(HEAD_DIM)) == hd, of[hd * nq:(hd + 1) * nq, :], oa)
    oa_ref[0] = oa.astype(BF16)

    s_past = jnp.dot(qd, pages_t(kd_refs), preferred_element_type=F32)
    key_pos = lax.broadcasted_iota(jnp.int32, (rows, past_len), 1).astype(F32)
    s_past = s_past + slope_row * (key_pos - (past_len + q_of_row))
    s_new = lax.dot_general(qd, padded(kb_ref), _NT, preferred_element_type=F32)
    s_new = jnp.where(visible, s_new + slope_row * (col.astype(F32) - q_of_row), NEG_INF)
    (p_past, p_new), l = _softmax_parts([s_past, s_new])
    vb_new = padded(vb_ref)
    per_head = 2 * nq
    heads = []
    for hd in range(H_DIFF):
        rs = slice(hd * per_head, (hd + 1) * per_head)
        v_past = jnp.concatenate([pr[pl.ds(hd, page, stride=H_DIFF), :] for pr in vd_refs],
                                 axis=0).astype(BF16)
        heads.append(jnp.dot(p_past[rs].astype(BF16), v_past, preferred_element_type=F32)
                     + jnp.dot(p_new[rs].astype(BF16), vb_new[:, hd * V_DIFF:(hd + 1) * V_DIFF],
                               preferred_element_type=F32))
    od = jnp.concatenate(heads, axis=0) / l
    lam = _diff_lambda(lam_ref, lambda_init)
    outs = []
    for hd in range(H_DIFF):
        r0 = hd * per_head
        o = od[r0:r0 + nq] - lam * od[r0 + nq:r0 + per_head]
        outs.append(_subln(o, gain_ref[...], lambda_init))
    ob_ref[0] = jnp.concatenate(outs, axis=1).astype(BF16)


def _decode_attention(page_table, caches_t, logf_t, layer, qa, ka, va, qb, kb, vb, cq_col, ck_new,
                      lam_vecs, gain, *, lambda_init):
    n_seq, nq, w = qa.shape
    n_pages = page_table.shape[1]
    hf, page = logf_t.shape[2:]
    wmat = jnp.asarray(_suffix_weights(page), dtype=BF16)

    def page_spec(cache, p):
        tail = cache.shape[2:]
        return pl.BlockSpec((None, None) + tail,
                            functools.partial(lambda b, pt, p: (layer, pt[b, p]) + (0,) * len(tail), p=p))

    def per_seq(shape):
        nd = len(shape)
        return pl.BlockSpec((1,) + tuple(shape[1:]), lambda b, pt: (b,) + (0,) * (nd - 1))

    paged = list(caches_t) + [logf_t]
    cache_specs = [page_spec(c, p) for c in paged for p in range(n_pages)]
    cache_args = [c for c in paged for _ in range(n_pages)]
    small = [qa, ka, va, qb, kb, vb, cq_col, ck_new]
    grid_spec = pltpu.PrefetchScalarGridSpec(
        num_scalar_prefetch=1,
        grid=(n_seq,),
        in_specs=cache_specs + [pl.BlockSpec(wmat.shape, lambda b, pt: (0, 0))]
        + [per_seq(a.shape) for a in small]
        + [pl.BlockSpec(lam_vecs.shape, lambda b, pt: (0, 0)),
           pl.BlockSpec(gain.shape, lambda b, pt: (0, 0))],
        out_specs=[per_seq(qa.shape), per_seq(qa.shape)],
        scratch_shapes=[pltpu.VMEM((n_pages, hf, page), F32)],
    )
    return pl.pallas_call(
        functools.partial(_decode_kernel, n_pages=n_pages, lambda_init=lambda_init),
        grid_spec=grid_spec,
        out_shape=[jax.ShapeDtypeStruct(qa.shape, BF16)] * 2,
        compiler_params=_params(("arbitrary",)),
        name="decode_attention",
    )(page_table, *cache_args, wmat, *small, lam_vecs, gain)


def _layer_norm(x, g, b):
    mu = jnp.mean(x, axis=-1, keepdims=True)
    xc = x - mu
    var = jnp.mean(xc * xc, axis=-1, keepdims=True)
    return xc * lax.rsqrt(var + LN_EPS) * g + b


def _ffn_kernel(x_ref, oa_ref, ob_ref, ga_ref, gb_ref, g1_ref, sh2_ref, sc2_ref, g2_ref,
                wa_ref, wb_ref, wo_ref, ln1g_ref, ln1b_ref, wg_ref, wu_ref, wd_ref,
                ln2g_ref, ln2b_ref, y_ref, *, alpha):
    nb, r, d = x_ref.shape
    rows = nb * r

    def flat(ref):
        return ref[...].reshape(rows, ref.shape[-1])

    def per_seq(ref):
        return jnp.broadcast_to(ref[...], (nb, r, d)).reshape(rows, d)

    a = jnp.dot(flat(oa_ref), wa_ref[...], preferred_element_type=F32)
    b = jnp.dot(flat(ob_ref), wb_ref[...], preferred_element_type=F32)
    m = flat(ga_ref).astype(F32) * a + flat(gb_ref).astype(F32) * b
    mo = jnp.dot(m.astype(BF16), wo_ref[...], preferred_element_type=F32)
    x1 = _layer_norm(alpha * flat(x_ref) + per_seq(g1_ref) * mo, ln1g_ref[...], ln1b_ref[...])
    h2 = (x1 * (1.0 + per_seq(sc2_ref)) + per_seq(sh2_ref)).astype(BF16)
    g = jnp.dot(h2, wg_ref[...], preferred_element_type=F32)
    u = jnp.dot(h2, wu_ref[...], preferred_element_type=F32)
    act = (g * jax.nn.sigmoid(g) * u).astype(BF16)
    f = jnp.dot(act, wd_ref[...], preferred_element_type=F32)
    y = _layer_norm(alpha * x1 + per_seq(g2_ref) * f, ln2g_ref[...], ln2b_ref[...])
    y_ref[...] = y.reshape(nb, r, d)


def _merge_ffn(x, oa, ob, ga, gb, g1, sh2, sc2, g2, weights, *, seqs_per_block, rows_per_seq_block,
               alpha):
    n_seq, t, d = x.shape
    nb, r = seqs_per_block, rows_per_seq_block

    def blk(width):
        return pl.BlockSpec((nb, r, width), lambda i, j: (i, j, 0))

    mod = pl.BlockSpec((nb, 1, d), lambda i, j: (i, 0, 0))
    return pl.pallas_call(
        functools.partial(_ffn_kernel, alpha=alpha),
        grid=(n_seq // nb, t // r),
        in_specs=[blk(d), blk(W_ATT), blk(W_ATT), blk(d), blk(d), mod, mod, mod, mod]
        + [_resident(wt.shape) for wt in weights],
        out_specs=blk(d),
        out_shape=jax.ShapeDtypeStruct((n_seq, t, d), F32),
        compiler_params=_params(("arbitrary", "arbitrary")),
        name="merge_ffn",
    )(x, oa, ob, ga, gb, g1, sh2, sc2, g2, *weights)


def kernel(x_prompt, x_sample, c_prompt, c_sample, cache_k_fox, cache_v_fox, cache_logf_fox,
           cache_k_diff, cache_v_diff, page_table, w_ada, b_ada, w_in, b_forget,
           lambda_q1, lambda_k1, lambda_q2, lambda_k2, subln_gain, w_branch_a, w_branch_b,
           w_out, ln1_gain, ln1_bias, w_ffn_gate, w_ffn_up, w_ffn_down, ln2_gain, ln2_bias):
    depth = w_in.shape[0]
    batch, seq, d = x_prompt.shape
    dec_batch, dec_seq, _ = x_sample.shape
    alpha = (2 * depth) ** 0.25
    w = W_ATT
    f_start = 3 * w
    caches_t = (jnp.transpose(cache_k_fox, (0, 1, 3, 4, 2)), jnp.transpose(cache_v_fox, (0, 1, 3, 4, 2)),
                jnp.transpose(cache_k_diff, (0, 1, 3, 4, 5, 2)),
                cache_v_diff.reshape(cache_v_diff.shape[:2] + (-1, V_DIFF)))
    logf_t = jnp.transpose(cache_logf_fox, (0, 1, 3, 2))

    xp, xs = x_prompt, x_sample
    outs_p = [[] for _ in range(5)]
    outs_s = [[] for _ in range(5)]
    for l in range(depth):
        lambda_init = 0.8 - 0.6 * math.exp(-0.3 * l)
        w_in_r = jnp.concatenate(
            [w_in[l][:, :f_start], w_in[l][:, f_start + H_FOX:], w_in[l][:, f_start:f_start + H_FOX],
             jnp.zeros((d, LANES - H_FOX), F32)], axis=1).astype(BF16)
        bf_pad = jnp.concatenate([b_forget[l], jnp.zeros((LANES - H_FOX,), F32)]).reshape(1, LANES)
        lam_vecs = jnp.stack([lambda_q1[l], lambda_k1[l], lambda_q2[l], lambda_k2[l]]).astype(F32)
        gain = subln_gain[l].reshape(1, V_DIFF)
        ffn_w = (w_branch_a[l].astype(BF16), w_branch_b[l].astype(BF16), w_out[l].astype(BF16),
                 ln1_gain[l].reshape(1, d), ln1_bias[l].reshape(1, d),
                 w_ffn_gate[l].astype(BF16), w_ffn_up[l].astype(BF16), w_ffn_down[l].astype(BF16),
                 ln2_gain[l].reshape(1, d), ln2_bias[l].reshape(1, d))

        mod = _adaln(jnp.concatenate([c_prompt, c_sample], axis=0), w_ada[l], b_ada[l])
        mod_p = [mod[k, :batch].reshape(batch, 1, d) for k in range(6)]
        mod_s = [mod[k, batch:].reshape(dec_batch, 1, d) for k in range(6)]

        sh1, sc1, g1, sh2, sc2, g2 = mod_p
        (qa, ka, va, qb, kb, vb, ga, gb, logf, ka_b, va_b, kb_b, vb_b, bias_q, bias_k) = _project(
            xp, sc1, sh1, w_in_r, bf_pad, seqs_per_block=1, rows_per_seq_block=PROJ_ROWS,
            group=seq, prompt=True)
        oa = _fox_prompt(qa, ka_b, va_b, bias_q, bias_k, blk=ATT_BLOCK)
        ob = _diff_prompt(qb, kb_b, vb_b, lam_vecs, gain, blk=ATT_BLOCK, lambda_init=lambda_init)
        xp = _merge_ffn(xp, oa, ob, ga, gb, g1, sh2, sc2, g2, ffn_w,
                        seqs_per_block=1, rows_per_seq_block=FFN_ROWS, alpha=alpha)
        for lst, val in zip(outs_p, (ka.reshape(batch, H_FOX, HEAD_DIM, seq).transpose(0, 3, 1, 2),
                                     va.reshape(batch, H_FOX, HEAD_DIM, seq).transpose(0, 3, 1, 2), logf,
                                     kb.reshape(batch, H_DIFF, 2, HEAD_DIM, seq).transpose(0, 4, 1, 2, 3),
                                     vb.reshape(batch, seq, H_DIFF, V_DIFF))):
            lst.append(val)

        sh1, sc1, g1, sh2, sc2, g2 = mod_s
        seqs_blk = PROJ_ROWS // dec_seq
        (qa, ka, va, qb, kb, vb, ga, gb, logf, cum) = _project(
            xs, sc1, sh1, w_in_r, bf_pad, seqs_per_block=seqs_blk, rows_per_seq_block=dec_seq,
            group=dec_seq, prompt=False)
        cum_hq = cum.transpose(0, 2, 1)
        cq_col = cum_hq.reshape(dec_batch, H_FOX * dec_seq, 1)
        ck_new = jnp.repeat(cum_hq, dec_seq, axis=1)
        ck_new = jnp.pad(ck_new, ((0, 0), (0, 0), (0, LANES - dec_seq)))
        oa, ob = _decode_attention(page_table, caches_t, logf_t, l, qa, ka, va, qb, kb, vb, cq_col, ck_new,
                                   lam_vecs, gain, lambda_init=lambda_init)
        ffn_seqs = FFN_ROWS // dec_seq
        xs = _merge_ffn(xs, oa, ob, ga, gb, g1, sh2, sc2, g2, ffn_w,
                        seqs_per_block=ffn_seqs, rows_per_seq_block=dec_seq, alpha=alpha)
        for lst, val in zip(outs_s, (ka.reshape(dec_batch, dec_seq, H_FOX, HEAD_DIM),
                                     va.reshape(dec_batch, dec_seq, H_FOX, HEAD_DIM), logf,
                                     kb.reshape(dec_batch, dec_seq, H_DIFF, 2, HEAD_DIM),
                                     vb.reshape(dec_batch, dec_seq, H_DIFF, V_DIFF))):
            lst.append(val)

    return (xp, xs, *[jnp.stack(o, 0) for o in outs_p], *[jnp.stack(o, 0) for o in outs_s])
```

```python
import functools
import math

import jax
import jax.numpy as jnp
import numpy as np
from jax import lax
from jax.experimental import pallas as pl
from jax.experimental.pallas import tpu as pltpu

F32 = jnp.float32
BF16 = jnp.bfloat16

LANES = 128
HEAD_DIM = 64
H_FOX = 8
H_DIFF = 4
V_DIFF = 2 * HEAD_DIM
W_ATT = H_FOX * HEAD_DIM
LN_EPS = 1e-5
NEG_INF = -1e30
LOG2E = math.log2(math.e)
VMEM_LIMIT = 56 * 1024 * 1024

PROJ_ROWS = 512
ATT_BLOCK = 512
FOX_HEADS_PER_LOOP = 8
DIFF_HEADS_PER_LOOP = 4
BIAS_LANES = 8
FFN_ROWS = 512

_NT = (((1,), (1,)), ((), ()))


def _log2(n):
    assert n & (n - 1) == 0, n
    return n.bit_length() - 1


def _resident(shape):
    nd = len(shape)
    return pl.BlockSpec(shape, lambda *_: (0,) * nd, pipeline_mode=pl.Buffered(1))


def _params(sem):
    return pltpu.CompilerParams(dimension_semantics=sem, vmem_limit_bytes=VMEM_LIMIT)


def _split3(x):
    hi = x.astype(BF16)
    r1 = x - hi.astype(F32)
    mid = r1.astype(BF16)
    lo = (r1 - mid.astype(F32)).astype(BF16)
    return hi, mid, lo


def _ada_kernel(c_ref, w_ref, b_ref, o_ref):
    o_ref[0] = jnp.dot(c_ref[...].astype(BF16), w_ref[...].astype(BF16),
                       preferred_element_type=F32) + b_ref[...]


def _adaln(c_all, w_ada, b_ada):
    m, d = c_all.shape
    n_chunks = w_ada.shape[1] // d
    return pl.pallas_call(
        _ada_kernel,
        grid=(n_chunks,),
        in_specs=[pl.BlockSpec((m, d), lambda j: (0, 0)),
                  pl.BlockSpec((d, d), lambda j: (0, j)),
                  pl.BlockSpec((1, d), lambda j: (0, j))],
        out_specs=pl.BlockSpec((1, m, d), lambda j: (j, 0, 0)),
        out_shape=jax.ShapeDtypeStruct((n_chunks, m, d), F32),
        compiler_params=_params(("arbitrary",)),
        name="adaln",
    )(c_all, w_ada, b_ada.reshape(1, -1))


def _fox_bias_map():
    e = np.zeros((3 * LANES, 2 * LANES), np.float32)
    ones = np.zeros((1, 2 * LANES), np.float32)
    for hd in range(H_FOX):
        for k in range(3):
            e[LANES * k + hd, BIAS_LANES * hd + k] = -1.0
            e[LANES * k + hd, LANES + BIAS_LANES * hd + 3 + k] = 1.0
            ones[0, BIAS_LANES * hd + 3 + k] = 1.0
            ones[0, LANES + BIAS_LANES * hd + k] = 1.0
    return e, ones


def _group_cumsum(x, group):
    rows = x.shape[0]
    tile = min(rows, LANES)
    assert group <= tile or group >= rows, (group, rows)
    ri = lax.broadcasted_iota(jnp.int32, (tile, tile), 0)
    ci = lax.broadcasted_iota(jnp.int32, (tile, tile), 1)
    gs = _log2(min(group, tile))
    tri = jnp.where((ci <= ri) & ((ri >> gs) == (ci >> gs)), 1.0, 0.0).astype(BF16)
    out, running = [], jnp.zeros((1, x.shape[1]), F32)
    for k in range(rows // tile):
        c = sum(jnp.dot(tri, piece, preferred_element_type=F32)
                for piece in _split3(x[k * tile:(k + 1) * tile]))
        if group >= rows:
            c = c + running
            running = c[tile - 1:tile]
        out.append(c)
    return jnp.concatenate(out, axis=0)


def _proj_kernel(x_ref, sc_ref, sh_ref, wa_ref, wb_ref, wf_ref, bf_ref, *rest, group, prompt,
                 fox_q_scale):
    if prompt:
        (emap_ref, eones_ref, qa_ref, ka_ref, va_ref, qb_ref, kb_ref, vb_ref, ga_ref, gb_ref, logf_ref,
         kab_ref, vab_ref, kbb_ref, vbb_ref, bq_ref, bk_ref, carry_ref) = rest
    else:
        (qa_ref, ka_ref, va_ref, qb_ref, kb_ref, vb_ref, ga_ref, gb_ref, logf_ref, cum_ref) = rest
    nb, r, d = x_ref.shape
    rows = nb * r
    h = x_ref[...] * (1.0 + sc_ref[...]) + sh_ref[...]
    hb = h.reshape(rows, d).astype(BF16)

    parts = ((0, wa_ref), (wa_ref.shape[1], wb_ref), (wa_ref.shape[1] + wb_ref.shape[1], wf_ref))

    def col(start, width):
        base, ref = [(b, p) for b, p in parts if b <= start][-1]
        return jnp.dot(hb, ref[:, start - base:start - base + width], preferred_element_type=F32)

    scale = HEAD_DIM ** -0.5
    w = W_ATT
    qa_ref[...] = (col(0, w) * fox_q_scale).astype(BF16).reshape(nb, r, w)
    ka = col(w, w)
    va = col(2 * w, w)
    qb_ref[...] = (col(3 * w, w) * scale).astype(BF16).reshape(nb, r, w)
    kb = col(4 * w, w)
    vb = col(5 * w, w)
    if prompt:
        ka_ref[0] = ka.T
        va_ref[0] = va.T
        kb_ref[0] = kb.T
        for hd in range(H_DIFF):
            vb_ref[0, pl.ds(hd, rows, stride=H_DIFF), :] = vb[:, hd * V_DIFF:(hd + 1) * V_DIFF]
        kab_ref[...] = ka.astype(BF16).reshape(nb, r, w)
        vab_ref[...] = va.astype(BF16).reshape(nb, r, w)
        kbb_ref[...] = kb.astype(BF16).reshape(nb, r, w)
        vbb_ref[...] = vb.astype(BF16).reshape(nb, r, w)
    else:
        ka_ref[...] = ka.reshape(nb, r, w)
        va_ref[...] = va.reshape(nb, r, w)
        kb_ref[...] = kb.reshape(nb, r, w)
        vb_ref[...] = vb.reshape(nb, r, w)
    ga_ref[...] = jax.nn.sigmoid(col(6 * w, d)).astype(BF16).reshape(nb, r, d)
    gb_ref[...] = jax.nn.sigmoid(col(6 * w + d, d)).astype(BF16).reshape(nb, r, d)

    fa = col(6 * w + 2 * d, LANES) + bf_ref[...]
    logf = jax.nn.log_sigmoid(fa)
    logf_ref[...] = logf[:, :H_FOX].reshape(nb, r, H_FOX)
    cum = _group_cumsum(logf, group)
    if not prompt:
        cum_ref[...] = cum[:, :H_FOX].reshape(nb, r, H_FOX)
        return
    @pl.when(pl.program_id(1) == 0)
    def _():
        carry_ref[...] = jnp.zeros_like(carry_ref)
    cum = cum + carry_ref[...]
    carry_ref[...] = cum[rows - 1:rows, :]
    pieces = jnp.concatenate(_split3(cum * LOG2E), axis=1)
    placed = jnp.dot(pieces, emap_ref[...], preferred_element_type=F32) + eones_ref[...]
    bk_ref[...] = placed[:, :LANES].astype(BF16).reshape(nb, r, LANES)
    bq_ref[...] = placed[:, LANES:].astype(BF16).reshape(nb, r, LANES)


def _project(x, sc, sh, w_parts, bf_pad, *, seqs_per_block, rows_per_seq_block, group, prompt):
    n_seq, t, d = x.shape
    nb, r = seqs_per_block, rows_per_seq_block
    grid = (n_seq // nb, t // r)
    w = W_ATT

    def blk(width):
        return pl.BlockSpec((nb, r, width), lambda i, j: (i, j, 0))

    def out(width, dtype):
        return jax.ShapeDtypeStruct((n_seq, t, width), dtype)

    mod = pl.BlockSpec((nb, 1, d), lambda i, j: (i, 0, 0))
    if prompt:
        kv_t = jax.ShapeDtypeStruct((n_seq, w, t), F32)
        kv_t_spec = pl.BlockSpec((nb, w, r), lambda i, j: (i, 0, j))
        vb_rows = jax.ShapeDtypeStruct((n_seq, t * H_DIFF, V_DIFF), F32)
        vb_rows_spec = pl.BlockSpec((nb, r * H_DIFF, V_DIFF), lambda i, j: (i, j, 0))
        kv_shapes = [kv_t, kv_t, kv_t, vb_rows]
        kv_specs = [kv_t_spec, kv_t_spec, kv_t_spec, vb_rows_spec]
    else:
        kv_shapes = [out(w, F32)] * 4
        kv_specs = [blk(w)] * 4
    out_shape = [out(w, BF16), kv_shapes[0], kv_shapes[1], out(w, BF16), kv_shapes[2], kv_shapes[3],
                 out(d, BF16), out(d, BF16), out(H_FOX, F32)]
    out_specs = [blk(w), kv_specs[0], kv_specs[1], blk(w), kv_specs[2], kv_specs[3],
                 blk(d), blk(d), blk(H_FOX)]
    inputs = [x, sc, sh, *w_parts, bf_pad]
    in_specs = [blk(d), mod, mod] + [_resident(wp.shape) for wp in w_parts] + [_resident(bf_pad.shape)]
    scratch = []
    if prompt:
        emap, eones = _fox_bias_map()
        inputs += [jnp.asarray(emap, dtype=BF16), jnp.asarray(eones)]
        in_specs += [_resident(emap.shape), _resident(eones.shape)]
        out_shape += [out(w, BF16)] * 4 + [out(LANES, BF16)] * 2
        out_specs += [blk(w)] * 4 + [blk(LANES)] * 2
        scratch = [pltpu.VMEM((1, LANES), F32)]
    else:
        out_shape += [out(H_FOX, F32)]
        out_specs += [blk(H_FOX)]
    return pl.pallas_call(
        functools.partial(_proj_kernel, group=group, prompt=prompt,
                          fox_q_scale=HEAD_DIM ** -0.5 * (LOG2E if prompt else 1.0)),
        grid=grid,
        in_specs=in_specs,
        out_specs=out_specs,
        out_shape=out_shape,
        scratch_shapes=scratch,
        compiler_params=_params(("arbitrary", "arbitrary")),
        name="project",
    )(*inputs)


def _online_update(s, m, l, acc, v):
    m_new = jnp.maximum(m, jnp.max(s, axis=-1, keepdims=True))
    alpha = jnp.exp(m - m_new)
    p = jnp.exp(s - m_new)
    l_new = alpha * l + jnp.sum(p, axis=-1, keepdims=True)
    acc_new = alpha * acc + jnp.dot(p.astype(BF16), v, preferred_element_type=F32)
    return m_new, l_new, acc_new


def _softmax_init(rows, width):
    return (jnp.full((rows, 1), NEG_INF, F32), jnp.zeros((rows, 1), F32), jnp.zeros((rows, width), F32))


def _online_update2(s, m, l, acc, v):
    m_new = jnp.maximum(m, jnp.max(s, axis=-1, keepdims=True))
    alpha = jnp.exp2(m - m_new)
    p = jnp.exp2(s - m_new)
    l_new = alpha * l + jnp.sum(p, axis=-1, keepdims=True)
    acc_new = alpha * acc + jnp.dot(p.astype(BF16), v, preferred_element_type=F32)
    return m_new, l_new, acc_new


def _fox_prompt_kernel(q_ref, k_ref, v_ref, bq_ref, ak_ref, o_ref, *, blk, heads_per_loop):
    i = pl.program_id(1)
    lane = lax.broadcasted_iota(jnp.int32, (blk, LANES), 1)
    lo = lane < HEAD_DIM
    ri = lax.broadcasted_iota(jnp.int32, (blk, blk), 0)
    ci = lax.broadcasted_iota(jnp.int32, (blk, blk), 1)
    causal = ci <= ri
    bq = bq_ref[0].astype(F32)
    lane_row = lax.broadcasted_iota(jnp.int32, (1, LANES), 1)
    keep_lanes = (jnp.where(lane_row < HEAD_DIM, 1.0, 0.0).astype(BF16),
                  jnp.where(lane_row < HEAD_DIM, 0.0, 1.0).astype(BF16))

    def pair_cols(hd):
        return slice((hd // 2) * LANES, (hd // 2 + 1) * LANES)

    def head_query(hd):
        q2 = q_ref[0, :, pair_cols(hd)].astype(F32)
        keep = lo if hd % 2 == 0 else jnp.logical_not(lo)
        own = (lane >> _log2(BIAS_LANES)) == hd
        return jnp.concatenate([jnp.where(keep, q2, 0.0), jnp.where(own, bq, 0.0)],
                               axis=1).astype(BF16)

    for first in range(0, H_FOX, heads_per_loop):
        heads = list(range(first, first + heads_per_loop))
        qs = [head_query(hd) for hd in heads]

        def step(j, carry, masked, heads=heads, qs=qs):
            start = pl.multiple_of(j * blk, blk)
            a_blk = ak_ref[0, pl.ds(start, blk), :]
            out = []
            for hd, qh, (m, acc) in zip(heads, qs, carry):
                ks = jnp.concatenate([k_ref[0, pl.ds(start, blk), pair_cols(hd)], a_blk], axis=1)
                vs = v_ref[0, pl.ds(start, blk), pair_cols(hd)] * keep_lanes[hd % 2] + keep_lanes[1 - hd % 2]
                s = lax.dot_general(qh, ks, _NT, preferred_element_type=F32)
                if masked:
                    s = jnp.where(causal, s, NEG_INF)
                m_new = jnp.maximum(m, jnp.max(s, axis=-1, keepdims=True))
                p = jnp.exp2(s - m_new).astype(BF16)
                acc = jnp.exp2(m - m_new) * acc + jnp.dot(p, vs, preferred_element_type=F32)
                out.append((m_new, acc))
            return tuple(out)

        init = tuple((jnp.full((blk, 1), NEG_INF, F32), jnp.zeros((blk, LANES), F32)) for _ in heads)
        carry = lax.fori_loop(0, i, functools.partial(step, masked=False), init)
        final = step(i, carry, True)
        for k in range(0, heads_per_loop, 2):
            (_, a0), (_, a1) = final[k], final[k + 1]
            o_ref[0, :, pair_cols(first + k)] = jnp.where(
                lo, a0 / pltpu.roll(a0, HEAD_DIM, 1), a1 / pltpu.roll(a1, HEAD_DIM, 1)).astype(BF16)


def _fox_prompt(q, k, v, bias_q, bias_k, *, blk):
    b, t, w = q.shape
    return pl.pallas_call(
        functools.partial(_fox_prompt_kernel, blk=blk, heads_per_loop=FOX_HEADS_PER_LOOP),
        grid=(b, t // blk),
        in_specs=[pl.BlockSpec((1, blk, w), lambda bi, i: (bi, i, 0)),
                  pl.BlockSpec((1, t, w), lambda bi, i: (bi, 0, 0)),
                  pl.BlockSpec((1, t, w), lambda bi, i: (bi, 0, 0)),
                  pl.BlockSpec((1, blk, LANES), lambda bi, i: (bi, i, 0)),
                  pl.BlockSpec((1, t, LANES), lambda bi, i: (bi, 0, 0))],
        out_specs=pl.BlockSpec((1, blk, w), lambda bi, i: (bi, i, 0)),
        out_shape=jax.ShapeDtypeStruct((b, t, w), BF16),
        compiler_params=_params(("arbitrary", "arbitrary")),
        name="fox_prompt",
    )(q, k, v, bias_q, bias_k)


def _diff_lambda(lam_ref, lambda_init):
    lv = lam_ref[...]
    s1 = jnp.sum(lv[0:1] * lv[1:2], axis=-1, keepdims=True)
    s2 = jnp.sum(lv[2:3] * lv[3:4], axis=-1, keepdims=True)
    return jnp.exp(s1) - jnp.exp(s2) + lambda_init


def _subln(o, gain, lambda_init):
    ms = jnp.mean(o * o, axis=-1, keepdims=True)
    return o * lax.rsqrt(ms + LN_EPS) * gain * (1.0 - lambda_init)


def _alibi_slope(hd):
    return 2.0 ** (-8.0 * (hd + 1) / H_DIFF)


def _alibi_key_lanes(t):
    s = np.arange(t)
    lanes = np.zeros((t, LANES), np.float32)
    lanes[:, 0], lanes[:, 1], lanes[:, 2], lanes[:, 3] = s >> 8, s & 255, 1.0, 1.0
    return lanes


def _diff_prompt_kernel(q_ref, k_ref, v_ref, pos_ref, lam_ref, gain_ref, o_ref, *, blk, lambda_init,
                        heads_per_loop):
    i = pl.program_id(1)
    lane = lax.broadcasted_iota(jnp.int32, (blk, LANES), 1)
    lo = lane < HEAD_DIM
    ri = lax.broadcasted_iota(jnp.int32, (blk, blk), 0)
    ci = lax.broadcasted_iota(jnp.int32, (blk, blk), 1)
    causal = ci <= ri
    lam = _diff_lambda(lam_ref, lambda_init)
    t = i * blk + lax.broadcasted_iota(jnp.int32, (blk, LANES), 0)
    t_hi = (t >> 8).astype(F32)
    t_lo = (t & 255).astype(F32)

    def head_queries(hd):
        slope = _alibi_slope(hd)
        pos = jnp.where(lane == 0, 256.0 * slope,
                        jnp.where(lane == 1, slope,
                                  jnp.where(lane == 2, (-256.0 * slope) * t_hi,
                                            jnp.where(lane == 3, -slope * t_lo, 0.0))))
        q2 = q_ref[0, :, hd * LANES:(hd + 1) * LANES].astype(F32)
        return (jnp.concatenate([jnp.where(lo, q2, 0.0), pos], axis=1).astype(BF16),
                jnp.concatenate([jnp.where(lo, 0.0, q2), pos], axis=1).astype(BF16))

    ones_tile = jnp.ones((blk, LANES), BF16)

    def update(s, m, acc, vs):
        m_new = jnp.maximum(m, jnp.max(s, axis=-1, keepdims=True))
        p = jnp.exp(s - m_new).astype(BF16)
        return m_new, jnp.exp(m - m_new) * acc + jnp.dot(p, vs, preferred_element_type=F32)

    for first in range(0, H_DIFF, heads_per_loop):
        heads = list(range(first, first + heads_per_loop))
        qmaps = [head_queries(hd) for hd in heads]

        def step(j, carry, masked, heads=heads, qmaps=qmaps):
            start = pl.multiple_of(j * blk, blk)
            pos_k = pos_ref[pl.ds(start, blk), :]
            out = []
            for hd, (q_1, q_2), (c1, c2) in zip(heads, qmaps, carry):
                cols = slice(hd * LANES, (hd + 1) * LANES)
                ks = jnp.concatenate([k_ref[0, pl.ds(start, blk), cols], pos_k], axis=1)
                vs = jnp.concatenate([v_ref[0, pl.ds(start, blk), cols], ones_tile], axis=1)
                s1 = lax.dot_general(q_1, ks, _NT, preferred_element_type=F32)
                s2 = lax.dot_general(q_2, ks, _NT, preferred_element_type=F32)
                if masked:
                    s1 = jnp.where(causal, s1, NEG_INF)
                    s2 = jnp.where(causal, s2, NEG_INF)
                out.append((update(s1, *c1, vs), update(s2, *c2, vs)))
            return tuple(out)

        def chain_init():
            return jnp.full((blk, 1), NEG_INF, F32), jnp.zeros((blk, 2 * LANES), F32)

        init = tuple((chain_init(), chain_init()) for _ in heads)
        carry = lax.fori_loop(0, i, functools.partial(step, masked=False), init)
        final = step(i, carry, True)
        for hd, ((_, a1), (_, a2)) in zip(heads, final):
            o = a1[:, :LANES] / a1[:, LANES:] - lam * (a2[:, :LANES] / a2[:, LANES:])
            o_ref[0, :, hd * LANES:(hd + 1) * LANES] = _subln(o, gain_ref[...], lambda_init).astype(BF16)


def _diff_prompt(q, k, v, lam_vecs, gain, *, blk, lambda_init):
    b, t, w = q.shape
    pos_k = jnp.asarray(_alibi_key_lanes(t), dtype=BF16)
    return pl.pallas_call(
        functools.partial(_diff_prompt_kernel, blk=blk, lambda_init=lambda_init,
                          heads_per_loop=DIFF_HEADS_PER_LOOP),
        grid=(b, t // blk),
        in_specs=[pl.BlockSpec((1, blk, w), lambda bi, i: (bi, i, 0)),
                  pl.BlockSpec((1, t, w), lambda bi, i: (bi, 0, 0)),
                  pl.BlockSpec((1, t, w), lambda bi, i: (bi, 0, 0)),
                  pl.BlockSpec(pos_k.shape, lambda bi, i: (0, 0)),
                  pl.BlockSpec(lam_vecs.shape, lambda bi, i: (0, 0)),
                  pl.BlockSpec(gain.shape, lambda bi, i: (0, 0))],
        out_specs=pl.BlockSpec((1, blk, w), lambda bi, i: (bi, i, 0)),
        out_shape=jax.ShapeDtypeStruct((b, t, w), BF16),
        compiler_params=_params(("arbitrary", "arbitrary")),
        name="diff_prompt",
    )(q, k, v, pos_k, lam_vecs, gain)


def _suffix_weights(page):
    t_in = np.arange(page)[:, None]
    suffix = t_in > np.arange(page)[None, :]
    return np.concatenate([suffix, np.ones_like(suffix)], axis=1).astype(np.float32)


def _suffix_sums(page_refs, w_ref, out_scr):
    n_pages = len(page_refs)
    hf, page = page_refs[0].shape
    lrow = jnp.concatenate([pr[...] for pr in page_refs], axis=0)
    wm = w_ref[...]
    y = sum(jnp.dot(piece, wm, preferred_element_type=F32) for piece in _split3(lrow))
    off = jnp.zeros((hf, page), F32)
    for p in reversed(range(n_pages)):
        out_scr[p] = y[p * hf:(p + 1) * hf, :page] + off
        off = off + y[p * hf:(p + 1) * hf, page:]


def _block_diag_queries(q):
    nq, w = q.shape
    groups = w // HEAD_DIM
    tiled = jnp.broadcast_to(q.astype(F32)[None], (groups, nq, w)).reshape(groups * nq, w)
    ri = lax.broadcasted_iota(jnp.int32, (groups * nq, w), 0)
    ci = lax.broadcasted_iota(jnp.int32, (groups * nq, w), 1)
    return jnp.where((ri >> _log2(nq)) == (ci >> _log2(HEAD_DIM)), tiled, 0.0).astype(BF16)


def _softmax_parts(scores):
    m = functools.reduce(jnp.maximum, [jnp.max(s, axis=-1, keepdims=True) for s in scores])
    ps = [jnp.exp(s - m) for s in scores]
    l = functools.reduce(jnp.add, [jnp.sum(p, axis=-1, keepdims=True) for p in ps])
    return ps, l


def _decode_kernel(pt_ref, *refs, n_pages, lambda_init):
    kf_refs = refs[0:n_pages]
    vf_refs = refs[n_pages:2 * n_pages]
    kd_refs = refs[2 * n_pages:3 * n_pages]
    vd_refs = refs[3 * n_pages:4 * n_pages]
    lf_refs = refs[4 * n_pages:5 * n_pages]
    (w_ref, qa_ref, ka_ref, va_ref, qb_ref, kb_ref, vb_ref, cq_ref, ckn_ref, lam_ref, gain_ref,
     oa_ref, ob_ref, r_scr) = refs[5 * n_pages:]
    nq = qa_ref.shape[1]
    page = kf_refs[0].shape[-1]
    rows = H_FOX * nq
    past_len = n_pages * page

    _suffix_sums(lf_refs, w_ref, r_scr)
    qf = _block_diag_queries(qa_ref[0])
    qd = _block_diag_queries(qb_ref[0])

    def pages_t(page_refs):
        return jnp.concatenate([pr[...].reshape(W_ATT, page) for pr in page_refs], axis=1).astype(BF16)

    pad = jnp.zeros((LANES - nq, W_ATT), F32)

    def padded(ref):
        return jnp.concatenate([ref[0], pad], axis=0).astype(BF16)

    row = lax.broadcasted_iota(jnp.int32, (rows, 1), 0)
    q_of_row = (row & (nq - 1)).astype(F32)
    slope_row = jnp.zeros((rows, 1), F32)
    for hd in range(H_DIFF):
        slope_row = jnp.where((row >> _log2(2 * nq)) == hd, _alibi_slope(hd), slope_row)
    col = lax.broadcasted_iota(jnp.int32, (rows, LANES), 1)
    visible = col <= (row & (nq - 1))

    s_past = jnp.dot(qf, pages_t(kf_refs), preferred_element_type=F32)
    r_tile = jnp.concatenate(
        [jnp.concatenate([jnp.broadcast_to(r_scr[p, hd:hd + 1, :], (nq, page)) for p in range(n_pages)],
                         axis=1) for hd in range(H_FOX)], axis=0)
    s_past = s_past + (cq_ref[0] + r_tile)
    s_new = lax.dot_general(qf, padded(ka_ref), _NT, preferred_element_type=F32)
    s_new = jnp.where(visible, s_new + (cq_ref[0] - ckn_ref[0]), NEG_INF)
    (p_past, p_new), l = _softmax_parts([s_past, s_new])
    acc = (lax.dot_general(p_past.astype(BF16), pages_t(vf_refs), _NT, preferred_element_type=F32)
           + jnp.dot(p_new.astype(BF16), padded(va_ref), preferred_element_type=F32))
    of = acc / l
    lane = lax.broadcasted_iota(jnp.int32, (nq, W_ATT), 1)
    oa = jnp.zeros((nq, W_ATT), F32)
    for hd in range(H_FOX):
        oa = jnp.where((lane >> _log2(HEAD_DIM)) == hd, of[hd * nq:(hd + 1) * nq, :], oa)
    oa_ref[0] = oa.astype(BF16)

    s_past = jnp.dot(qd, pages_t(kd_refs), preferred_element_type=F32)
    key_pos = lax.broadcasted_iota(jnp.int32, (rows, past_len), 1).astype(F32)
    s_past = s_past + slope_row * (key_pos - (past_len + q_of_row))
    s_new = lax.dot_general(qd, padded(kb_ref), _NT, preferred_element_type=F32)
    s_new = jnp.where(visible, s_new + slope_row * (col.astype(F32) - q_of_row), NEG_INF)
    (p_past, p_new), l = _softmax_parts([s_past, s_new])
    vb_new = padded(vb_ref)
    per_head = 2 * nq
    heads = []
    for hd in range(H_DIFF):
        rs = slice(hd * per_head, (hd + 1) * per_head)
        v_past = jnp.concatenate([pr[pl.ds(hd, page, stride=H_DIFF), :] for pr in vd_refs],
                                 axis=0).astype(BF16)
        heads.append(jnp.dot(p_past[rs].astype(BF16), v_past, preferred_element_type=F32)
                     + jnp.dot(p_new[rs].astype(BF16), vb_new[:, hd * V_DIFF:(hd + 1) * V_DIFF],
                               preferred_element_type=F32))
    od = jnp.concatenate(heads, axis=0) / l
    lam = _diff_lambda(lam_ref, lambda_init)
    outs = []
    for hd in range(H_DIFF):
        r0 = hd * per_head
        o = od[r0:r0 + nq] - lam * od[r0 + nq:r0 + per_head]
        outs.append(_subln(o, gain_ref[...], lambda_init))
    ob_ref[0] = jnp.concatenate(outs, axis=1).astype(BF16)


def _decode_attention(page_table, caches_t, logf_t, layer, qa, ka, va, qb, kb, vb, cq_col, ck_new,
                      lam_vecs, gain, *, lambda_init):
    n_seq, nq, w = qa.shape
    n_pages = page_table.shape[1]
    hf, page = logf_t.shape[2:]
    wmat = jnp.asarray(_suffix_weights(page), dtype=BF16)

    def page_spec(cache, p):
        tail = cache.shape[2:]
        return pl.BlockSpec((None, None) + tail,
                            functools.partial(lambda b, pt, p: (layer, pt[b, p]) + (0,) * len(tail), p=p))

    def per_seq(shape):
        nd = len(shape)
        return pl.BlockSpec((1,) + tuple(shape[1:]), lambda b, pt: (b,) + (0,) * (nd - 1))

    paged = list(caches_t) + [logf_t]
    cache_specs = [page_spec(c, p) for c in paged for p in range(n_pages)]
    cache_args = [c for c in paged for _ in range(n_pages)]
    small = [qa, ka, va, qb, kb, vb, cq_col, ck_new]
    grid_spec = pltpu.PrefetchScalarGridSpec(
        num_scalar_prefetch=1,
        grid=(n_seq,),
        in_specs=cache_specs + [pl.BlockSpec(wmat.shape, lambda b, pt: (0, 0))]
        + [per_seq(a.shape) for a in small]
        + [pl.BlockSpec(lam_vecs.shape, lambda b, pt: (0, 0)),
           pl.BlockSpec(gain.shape, lambda b, pt: (0, 0))],
        out_specs=[per_seq(qa.shape), per_seq(qa.shape)],
        scratch_shapes=[pltpu.VMEM((n_pages, hf, page), F32)],
    )
    return pl.pallas_call(
        functools.partial(_decode_kernel, n_pages=n_pages, lambda_init=lambda_init),
        grid_spec=grid_spec,
        out_shape=[jax.ShapeDtypeStruct(qa.shape, BF16)] * 2,
        compiler_params=_params(("arbitrary",)),
        name="decode_attention",
    )(page_table, *cache_args, wmat, *small, lam_vecs, gain)


def _layer_norm(x, g, b):
    mu = jnp.mean(x, axis=-1, keepdims=True)
    xc = x - mu
    var = jnp.mean(xc * xc, axis=-1, keepdims=True)
    return xc * lax.rsqrt(var + LN_EPS) * g + b


def _ffn_kernel(x_ref, oa_ref, ob_ref, ga_ref, gb_ref, g1_ref, sh2_ref, sc2_ref, g2_ref,
                wa_ref, wb_ref, wo_ref, ln1g_ref, ln1b_ref, wg_ref, wu_ref, wd_ref,
                ln2g_ref, ln2b_ref, y_ref, *, alpha):
    nb, r, d = x_ref.shape
    rows = nb * r

    def flat(ref):
        return ref[...].reshape(rows, ref.shape[-1])

    def per_seq(ref):
        return jnp.broadcast_to(ref[...], (nb, r, d)).reshape(rows, d)

    a = jnp.dot(flat(oa_ref), wa_ref[...], preferred_element_type=F32)
    b = jnp.dot(flat(ob_ref), wb_ref[...], preferred_element_type=F32)
    m = flat(ga_ref).astype(F32) * a + flat(gb_ref).astype(F32) * b
    mo = jnp.dot(m.astype(BF16), wo_ref[...], preferred_element_type=F32)
    x1 = _layer_norm(alpha * flat(x_ref) + per_seq(g1_ref) * mo, ln1g_ref[...], ln1b_ref[...])
    h2 = (x1 * (1.0 + per_seq(sc2_ref)) + per_seq(sh2_ref)).astype(BF16)
    g = jnp.dot(h2, wg_ref[...], preferred_element_type=F32)
    u = jnp.dot(h2, wu_ref[...], preferred_element_type=F32)
    act = (g * jax.nn.sigmoid(g) * u).astype(BF16)
    f = jnp.dot(act, wd_ref[...], preferred_element_type=F32)
    y = _layer_norm(alpha * x1 + per_seq(g2_ref) * f, ln2g_ref[...], ln2b_ref[...])
    y_ref[...] = y.reshape(nb, r, d)


def _merge_ffn(x, oa, ob, ga, gb, g1, sh2, sc2, g2, weights, *, seqs_per_block, rows_per_seq_block,
               alpha):
    n_seq, t, d = x.shape
    nb, r = seqs_per_block, rows_per_seq_block

    def blk(width):
        return pl.BlockSpec((nb, r, width), lambda i, j: (i, j, 0))

    mod = pl.BlockSpec((nb, 1, d), lambda i, j: (i, 0, 0))
    return pl.pallas_call(
        functools.partial(_ffn_kernel, alpha=alpha),
        grid=(n_seq // nb, t // r),
        in_specs=[blk(d), blk(W_ATT), blk(W_ATT), blk(d), blk(d), mod, mod, mod, mod]
        + [_resident(wt.shape) for wt in weights],
        out_specs=blk(d),
        out_shape=jax.ShapeDtypeStruct((n_seq, t, d), F32),
        compiler_params=_params(("arbitrary", "arbitrary")),
        name="merge_ffn",
    )(x, oa, ob, ga, gb, g1, sh2, sc2, g2, *weights)


def kernel(x_prompt, x_sample, c_prompt, c_sample, cache_k_fox, cache_v_fox, cache_logf_fox,
           cache_k_diff, cache_v_diff, page_table, w_ada, b_ada, w_in, b_forget,
           lambda_q1, lambda_k1, lambda_q2, lambda_k2, subln_gain, w_branch_a, w_branch_b,
           w_out, ln1_gain, ln1_bias, w_ffn_gate, w_ffn_up, w_ffn_down, ln2_gain, ln2_bias):
    depth = w_in.shape[0]
    batch, seq, d = x_prompt.shape
    dec_batch, dec_seq, _ = x_sample.shape
    alpha = (2 * depth) ** 0.25
    w = W_ATT
    f_start = 3 * w
    caches_t = (jnp.transpose(cache_k_fox, (0, 1, 3, 4, 2)), jnp.transpose(cache_v_fox, (0, 1, 3, 4, 2)),
                jnp.transpose(cache_k_diff, (0, 1, 3, 4, 5, 2)),
                cache_v_diff.reshape(cache_v_diff.shape[:2] + (-1, V_DIFF)))
    logf_t = jnp.transpose(cache_logf_fox, (0, 1, 3, 2))

    xp, xs = x_prompt, x_sample
    outs_p = [[] for _ in range(5)]
    outs_s = [[] for _ in range(5)]
    for l in range(depth):
        lambda_init = 0.8 - 0.6 * math.exp(-0.3 * l)
        w_bf = w_in[l].astype(BF16)
        w_parts = (w_bf[:, :f_start], w_bf[:, f_start + H_FOX:],
                   jnp.pad(w_bf[:, f_start:f_start + H_FOX], ((0, 0), (0, LANES - H_FOX))))
        bf_pad = jnp.concatenate([b_forget[l], jnp.zeros((LANES - H_FOX,), F32)]).reshape(1, LANES)
        lam_vecs = jnp.stack([lambda_q1[l], lambda_k1[l], lambda_q2[l], lambda_k2[l]]).astype(F32)
        gain = subln_gain[l].reshape(1, V_DIFF)
        ffn_w = (w_branch_a[l].astype(BF16), w_branch_b[l].astype(BF16), w_out[l].astype(BF16),
                 ln1_gain[l].reshape(1, d), ln1_bias[l].reshape(1, d),
                 w_ffn_gate[l].astype(BF16), w_ffn_up[l].astype(BF16), w_ffn_down[l].astype(BF16),
                 ln2_gain[l].reshape(1, d), ln2_bias[l].reshape(1, d))

        mod = _adaln(jnp.concatenate([c_prompt, c_sample], axis=0), w_ada[l], b_ada[l])
        mod_p = [mod[k, :batch].reshape(batch, 1, d) for k in range(6)]
        mod_s = [mod[k, batch:].reshape(dec_batch, 1, d) for k in range(6)]

        sh1, sc1, g1, sh2, sc2, g2 = mod_p
        (qa, ka, va, qb, kb, vb, ga, gb, logf, ka_b, va_b, kb_b, vb_b, bias_q, bias_k) = _project(
            xp, sc1, sh1, w_parts, bf_pad, seqs_per_block=1, rows_per_seq_block=PROJ_ROWS,
            group=seq, prompt=True)
        oa = _fox_prompt(qa, ka_b, va_b, bias_q, bias_k, blk=ATT_BLOCK)
        ob = _diff_prompt(qb, kb_b, vb_b, lam_vecs, gain, blk=ATT_BLOCK, lambda_init=lambda_init)
        xp = _merge_ffn(xp, oa, ob, ga, gb, g1, sh2, sc2, g2, ffn_w,
                        seqs_per_block=1, rows_per_seq_block=FFN_ROWS, alpha=alpha)
        for lst, val in zip(outs_p, (ka.reshape(batch, H_FOX, HEAD_DIM, seq).transpose(0, 3, 1, 2),
                                     va.reshape(batch, H_FOX, HEAD_DIM, seq).transpose(0, 3, 1, 2), logf,
                                     kb.reshape(batch, H_DIFF, 2, HEAD_DIM, seq).transpose(0, 4, 1, 2, 3),
                                     vb.reshape(batch, seq, H_DIFF, V_DIFF))):
            lst.append(val)

        sh1, sc1, g1, sh2, sc2, g2 = mod_s
        seqs_blk = PROJ_ROWS // dec_seq
        (qa, ka, va, qb, kb, vb, ga, gb, logf, cum) = _project(
            xs, sc1, sh1, w_parts, bf_pad, seqs_per_block=seqs_blk, rows_per_seq_block=dec_seq,
            group=dec_seq, prompt=False)
        cum_hq = cum.transpose(0, 2, 1)
        cq_col = cum_hq.reshape(dec_batch, H_FOX * dec_seq, 1)
        ck_new = jnp.repeat(cum_hq, dec_seq, axis=1)
        ck_new = jnp.pad(ck_new, ((0, 0), (0, 0), (0, LANES - dec_seq)))
        oa, ob = _decode_attention(page_table, caches_t, logf_t, l, qa, ka, va, qb, kb, vb, cq_col, ck_new,
                                   lam_vecs, gain, lambda_init=lambda_init)
        ffn_seqs = FFN_ROWS // dec_seq
        xs = _merge_ffn(xs, oa, ob, ga, gb, g1, sh2, sc2, g2, ffn_w,
                        seqs_per_block=ffn_seqs, rows_per_seq_block=dec_seq, alpha=alpha)
        for lst, val in zip(outs_s, (ka.reshape(dec_batch, dec_seq, H_FOX, HEAD_DIM),
                                     va.reshape(dec_batch, dec_seq, H_FOX, HEAD_DIM), logf,
                                     kb.reshape(dec_batch, dec_seq, H_DIFF, 2, HEAD_DIM),
                                     vb.reshape(dec_batch, dec_seq, H_DIFF, V_DIFF))):
            lst.append(val)

    return (xp, xs, *[jnp.stack(o, 0) for o in outs_p], *[jnp.stack(o, 0) for o in outs_s])
```
